```python
import math
import jax, jax.numpy as jnp
from jax import lax
import numpy as np

D_MODEL = 1024
BATCH = 8
SEQ = 2048
DEPTH = 2
DEC_BATCH = 128
DEC_SEQ = 8
PAST_LEN = 16384
PAGE_SIZE = 128

MIX_WIDTH = D_MODEL
HGRN_HEADS = 4
HGRN_DV = MIX_WIDTH // 2 // HGRN_HEADS
HGRN_DK = HGRN_DV
RET_HEADS = 4
RET_DV = MIX_WIDTH // 2 // RET_HEADS
RET_DK = RET_DV // 2
HGRN_KW = HGRN_HEADS * HGRN_DK
HGRN_VW = HGRN_HEADS * HGRN_DV
RET_KW = RET_HEADS * RET_DK
RET_VW = RET_HEADS * RET_DV
IN_SPLITS = (HGRN_KW, 2 * HGRN_KW, 2 * HGRN_KW + HGRN_VW, 2 * HGRN_KW + 2 * HGRN_VW,
             2 * HGRN_KW + 2 * HGRN_VW + RET_KW, 2 * HGRN_KW + 2 * HGRN_VW + 2 * RET_KW,
             2 * HGRN_KW + 2 * HGRN_VW + 2 * RET_KW + RET_VW)
D_IN = 2 * HGRN_KW + 2 * HGRN_VW + 2 * RET_KW + 2 * RET_VW
HGRN_CHUNK = 16
RET_CHUNK = 64
ROPE_BASE = 10000.0
N_EXPERTS = 32
TOP_K = 4
D_FF_EXPERT = D_MODEL
SWIGLU_LIMIT = 7.0
SWIGLU_ALPHA = 1.702
MOE_BLOCK = 128
EPS = 1e-6

kernel_name = 'hymba_hgrn2_retnet_moe_decode_step'


def _rmsnorm(x, g):
    xf = x.astype(jnp.float32)
    y = xf * lax.rsqrt(jnp.mean(xf * xf, axis=-1, keepdims=True) + EPS)
    return (y * g.astype(jnp.float32)).astype(x.dtype)


def _modulate(h, shift, scale):
    return h * (1.0 + scale[:, None, :]) + shift[:, None, :]


def _headnorm(o, g):
    B, T, H, dv = o.shape
    o = o * lax.rsqrt(jnp.mean(o * o, axis=-1, keepdims=True) + EPS)
    return o.reshape(B, T, H * dv) * g.astype(jnp.float32)


def _to_chunks(x, C):
    B, T, H, d = x.shape
    return x.reshape(B, T // C, C, H, d).transpose(1, 0, 3, 2, 4)


def _from_chunks(o):
    N, B, H, C, d = o.shape
    return o.transpose(1, 0, 3, 2, 4).reshape(B, N * C, H, d)


def _gla_chunkwise(q, k, v, logf, s0):
    T = q.shape[1]
    C = math.gcd(T, HGRN_CHUNK)
    causal = jnp.tril(jnp.ones((C, C), dtype=bool))

    def step(S, inp):
        qc, kc, vc, gc = inp
        b = jnp.cumsum(gc, axis=2)
        o = jnp.einsum('bhtd,bhdv->bhtv', qc * jnp.exp(b), S)
        diff = b[:, :, :, None, :] - b[:, :, None, :, :]
        dec = jnp.exp(jnp.where(causal[:, :, None], diff, -jnp.inf))
        att = jnp.einsum('bhtd,bhsd,bhtsd->bhts', qc, kc, dec)
        o = o + jnp.einsum('bhts,bhsv->bhtv', att, vc)
        b_last = b[:, :, -1:, :]
        S = jnp.exp(b_last[:, :, 0, :])[..., None] * S + jnp.einsum(
            'bhsd,bhsv->bhdv', kc * jnp.exp(b_last - b), vc)
        return S, o

    S, o = lax.scan(step, s0, (_to_chunks(q, C), _to_chunks(k, C), _to_chunks(v, C), _to_chunks(logf, C)))
    return _from_chunks(o), S


def _retention_chunkwise(q, k, v, s0):
    T = q.shape[1]
    C = math.gcd(T, RET_CHUNK)
    log_gamma = jnp.log(1.0 - jnp.exp2(-5.0 - jnp.arange(RET_HEADS, dtype=jnp.float32)))
    t = jnp.arange(C, dtype=jnp.float32)
    rel = t[:, None] - t[None, :]
    decay = jnp.where(rel >= 0, jnp.exp(log_gamma[:, None, None] * jnp.maximum(rel, 0.0)), 0.0)
    q_decay = jnp.exp(log_gamma[:, None] * (t + 1.0))
    k_decay = jnp.exp(log_gamma[:, None] * (C - 1.0 - t))
    s_decay = jnp.exp(log_gamma * C)

    def step(S, inp):
        qc, kc, vc = inp
        o = jnp.einsum('bhtd,bhdv->bhtv', qc, S) * q_decay[None, :, :, None]
        att = jnp.einsum('bhtd,bhsd->bhts', qc, kc) * decay[None]
        o = o + jnp.einsum('bhts,bhsv->bhtv', att, vc)
        S = s_decay[None, :, None, None] * S + jnp.einsum(
            'bhsd,bhsv->bhdv', kc * k_decay[None, :, :, None], vc)
        return S, o

    S, o = lax.scan(step, s0, (_to_chunks(q, C), _to_chunks(k, C), _to_chunks(v, C)))
    return _from_chunks(o), S


def _rotary(x, pos):
    half = RET_DK // 2
    theta = 1.0 / (ROPE_BASE ** jnp.linspace(0.0, 1.0, half, dtype=jnp.float32))
    ang = pos[:, None] * theta[None, :]
    cos = jnp.cos(ang)[None, :, None, :]
    sin = jnp.sin(ang)[None, :, None, :]
    x1, x2 = x[..., :half], x[..., half:]
    return jnp.concatenate([x1 * cos - x2 * sin, x2 * cos + x1 * sin], axis=-1)


def _hgrn2(h_q, h_f, h_i, h_g, lb, gn, s0):
    B, T, _ = h_q.shape
    shk = (B, T, HGRN_HEADS, HGRN_DK)
    q = jax.nn.silu(h_q.astype(jnp.float32)).reshape(shk)
    f = lb + (1.0 - lb) * jax.nn.sigmoid(h_f.astype(jnp.float32))
    logf = jnp.log(f).reshape(shk)
    k = (1.0 - f).reshape(shk)
    v = h_i.astype(jnp.float32).reshape(B, T, HGRN_HEADS, HGRN_DV)
    o, s = _gla_chunkwise(q, k, v, logf, s0.astype(jnp.float32))
    o = _headnorm(o, gn) * jax.nn.sigmoid(h_g.astype(jnp.float32))
    return o, s


def _retention(h_q, h_k, h_v, h_g, gn, s0, pos0):
    B, T, _ = h_q.shape
    shk = (B, T, RET_HEADS, RET_DK)
    pos = pos0 + jnp.arange(T, dtype=jnp.float32)
    q = _rotary(h_q.astype(jnp.float32).reshape(shk), pos)
    k = _rotary(h_k.astype(jnp.float32).reshape(shk), pos) * (RET_DK ** -0.5)
    v = h_v.astype(jnp.float32).reshape(B, T, RET_HEADS, RET_DV)
    o, s = _retention_chunkwise(q, k, v, s0.astype(jnp.float32))
    o = _headnorm(o, gn) * jax.nn.silu(h_g.astype(jnp.float32))
    return o, s


def _moe(h, router_w, router_b, w_gu, b_gu, w_dn, b_dn):
    M, D = h.shape
    logits = h.astype(jnp.float32) @ router_w.astype(jnp.float32) + router_b.astype(jnp.float32)
    top_v, top_i = lax.top_k(logits, TOP_K)
    gates = jax.nn.softmax(top_v, axis=-1)
    A = M * TOP_K
    flat_e = top_i.reshape(-1).astype(jnp.int32)
    flat_g = gates.reshape(-1)
    counts = jnp.zeros((N_EXPERTS,), jnp.int32).at[flat_e].add(1)
    pad_counts = (counts + MOE_BLOCK - 1) // MOE_BLOCK * MOE_BLOCK
    start = jnp.cumsum(counts) - counts
    pad_end = jnp.cumsum(pad_counts)
    pad_start = pad_end - pad_counts
    order = jnp.argsort(flat_e, stable=True).astype(jnp.int32)
    e_sorted = flat_e[order]
    dest = pad_start[e_sorted] + jnp.arange(A, dtype=jnp.int32) - start[e_sorted]
    n_blocks = (A + N_EXPERTS * (MOE_BLOCK - 1) + MOE_BLOCK - 1) // MOE_BLOCK
    P = n_blocks * MOE_BLOCK
    slot_tok = jnp.full((P,), M, jnp.int32).at[dest].set(order // TOP_K)
    slot_gate = jnp.zeros((P,), jnp.float32).at[dest].set(flat_g[order])
    block_start = jnp.arange(n_blocks, dtype=jnp.int32) * MOE_BLOCK
    block_expert = jnp.minimum(jnp.sum(block_start[:, None] >= pad_end[None, :], axis=1),
                               N_EXPERTS - 1).astype(jnp.int32)
    h_pad = jnp.concatenate([h, jnp.zeros((1, D), h.dtype)], axis=0)
    xb = h_pad[slot_tok].reshape(n_blocks, MOE_BLOCK, D)

    def expert_block(args):
        e, xe = args
        gu = xe @ w_gu[e] + b_gu[e]
        gate, up = jnp.split(gu, 2, axis=-1)
        gate = jnp.minimum(gate, SWIGLU_LIMIT)
        up = jnp.clip(up, -SWIGLU_LIMIT, SWIGLU_LIMIT)
        glu = gate * jax.nn.sigmoid(SWIGLU_ALPHA * gate)
        return ((up + 1.0) * glu) @ w_dn[e] + b_dn[e]

    yb = lax.map(expert_block, (block_expert, xb)).reshape(P, D)
    out = jnp.zeros((M + 1, D), jnp.float32).at[slot_tok].add(yb.astype(jnp.float32) * slot_gate[:, None])
    return out[:M].astype(h.dtype)


def _trunk(x, c, s_hgrn, s_ret, pos0, ada_w, ada_b, norm1_g, norm2_g, w_in, lb_logits,
           gn_hgrn, gn_ret, w_out, router_w, router_b, w_gate_up, b_gate_up, w_down, b_down,
           final_g, final_ada_w, final_ada_b):
    B, T, D = x.shape
    lb_sm = jax.nn.softmax(lb_logits.astype(jnp.float32), axis=0)
    lb_all = jnp.cumsum(lb_sm, axis=0) - lb_sm[0]
    cs = jax.nn.silu(c)
    new_a, new_r = [], []
    for l in range(DEPTH):
        mod = cs @ ada_w[l] + ada_b[l]
        sh1, sc1, g1, sh2, sc2, g2 = jnp.split(mod, 6, axis=-1)
        h = _modulate(_rmsnorm(x, norm1_g[l]), sh1, sc1)
        proj = h @ w_in[l]
        qa, fa, ia, ga, qr, kr, vr, gr = jnp.split(proj, IN_SPLITS, axis=-1)
        oa, sa = _hgrn2(qa, fa, ia, ga, lb_all[l], gn_hgrn[l], s_hgrn[l])
        orr, sr = _retention(qr, kr, vr, gr, gn_ret[l], s_ret[l], pos0)
        mix = jnp.concatenate([oa, orr], axis=-1).astype(x.dtype) @ w_out[l]
        x = x + g1[:, None, :] * mix
        h2 = _modulate(_rmsnorm(x, norm2_g[l]), sh2, sc2)
        ff = _moe(h2.reshape(B * T, D), router_w[l], router_b[l], w_gate_up[l], b_gate_up[l],
                  w_down[l], b_down[l]).reshape(B, T, D)
        x = x + g2[:, None, :] * ff
        new_a.append(sa)
        new_r.append(sr)
    fmod = cs @ final_ada_w + final_ada_b
    fsh, fsc = jnp.split(fmod, 2, axis=-1)
    y = _modulate(_rmsnorm(x, final_g), fsh, fsc)
    return y, jnp.stack(new_a), jnp.stack(new_r)


def setup_inputs(seed: int = 0) -> dict:
    key = jax.random.key(seed)
    ks = jax.random.split(key, 24)
    f32 = jnp.float32
    nrm = lambda k, shape, s: jax.random.normal(k, shape, f32) * s
    return {
        'x_prompt': nrm(ks[0], (BATCH, SEQ, D_MODEL), 1.0),
        'x_sample': nrm(ks[1], (DEC_BATCH, DEC_SEQ, D_MODEL), 1.0),
        'c_prompt': nrm(ks[2], (BATCH, D_MODEL), 1.0),
        'c_sample': nrm(ks[3], (DEC_BATCH, D_MODEL), 1.0),
        'state_hgrn': nrm(ks[4], (DEPTH, DEC_BATCH, HGRN_HEADS, HGRN_DK, HGRN_DV), 0.5),
        'state_ret': nrm(ks[5], (DEPTH, DEC_BATCH, RET_HEADS, RET_DK, RET_DV), 1.0),
        'ada_w': nrm(ks[6], (DEPTH, D_MODEL, 6 * D_MODEL), 0.5 * D_MODEL ** -0.5),
        'ada_b': nrm(ks[7], (DEPTH, 6 * D_MODEL), 0.02),
        'norm1_g': 1.0 + nrm(ks[8], (DEPTH, D_MODEL), 0.02),
        'norm2_g': 1.0 + nrm(ks[9], (DEPTH, D_MODEL), 0.02),
        'w_in': nrm(ks[10], (DEPTH, D_MODEL, D_IN), D_MODEL ** -0.5),
        'lb_logits': nrm(ks[11], (DEPTH, HGRN_KW), 0.5),
        'gn_hgrn': 1.0 + nrm(ks[12], (DEPTH, HGRN_VW), 0.02),
        'gn_ret': 1.0 + nrm(ks[13], (DEPTH, RET_VW), 0.02),
        'w_out': nrm(ks[14], (DEPTH, MIX_WIDTH, D_MODEL), MIX_WIDTH ** -0.5),
        'router_w': nrm(ks[15], (DEPTH, D_MODEL, N_EXPERTS), D_MODEL ** -0.5),
        'router_b': nrm(ks[16], (DEPTH, N_EXPERTS), 0.01),
        'w_gate_up': nrm(ks[17], (DEPTH, N_EXPERTS, D_MODEL, 2 * D_FF_EXPERT), D_MODEL ** -0.5),
        'b_gate_up': nrm(ks[18], (DEPTH, N_EXPERTS, 2 * D_FF_EXPERT), 0.02),
        'w_down': nrm(ks[19], (DEPTH, N_EXPERTS, D_FF_EXPERT, D_MODEL), D_FF_EXPERT ** -0.5),
        'b_down': nrm(ks[20], (DEPTH, N_EXPERTS, D_MODEL), 0.02),
        'final_g': 1.0 + nrm(ks[21], (D_MODEL,), 0.02),
        'final_ada_w': nrm(ks[22], (D_MODEL, 2 * D_MODEL), 0.5 * D_MODEL ** -0.5),
        'final_ada_b': nrm(ks[23], (2 * D_MODEL,), 0.02),
    }


def reference(x_prompt, x_sample, c_prompt, c_sample, state_hgrn, state_ret, ada_w, ada_b,
              norm1_g, norm2_g, w_in, lb_logits, gn_hgrn, gn_ret, w_out, router_w, router_b,
              w_gate_up, b_gate_up, w_down, b_down, final_g, final_ada_w, final_ada_b):
    weights = (ada_w, ada_b, norm1_g, norm2_g, w_in, lb_logits, gn_hgrn, gn_ret, w_out,
               router_w, router_b, w_gate_up, b_gate_up, w_down, b_down,
               final_g, final_ada_w, final_ada_b)
    n_prompt = x_prompt.shape[0]
    zero_a = jnp.zeros((DEPTH, n_prompt, HGRN_HEADS, HGRN_DK, HGRN_DV), jnp.float32)
    zero_r = jnp.zeros((DEPTH, n_prompt, RET_HEADS, RET_DK, RET_DV), jnp.float32)
    y_prompt, hgrn_prompt, ret_prompt = _trunk(x_prompt, c_prompt, zero_a, zero_r, 0, *weights)
    y_sample, hgrn_sample, ret_sample = _trunk(x_sample, c_sample, state_hgrn, state_ret, PAST_LEN, *weights)
    return (y_prompt, y_sample, hgrn_prompt.astype(x_prompt.dtype), ret_prompt.astype(x_prompt.dtype),
            hgrn_sample.astype(state_hgrn.dtype), ret_sample.astype(state_ret.dtype))
```

```python
import functools
import math

import jax
import jax.numpy as jnp
from jax import lax
from jax.experimental import pallas as pl
from jax.experimental.pallas import tpu as pltpu

F32 = jnp.float32
BF16 = jnp.bfloat16
HIGHEST = lax.Precision.HIGHEST

D_MODEL = 1024
LANES = 128
ROW_TILES = D_MODEL // LANES
N_HEADS = 4
HGRN_DK = 128
HGRN_W = N_HEADS * HGRN_DK
RET_DK = 64
RET_DV = 128
RET_KW = N_HEADS * RET_DK
RET_VW = N_HEADS * RET_DV
D_IN = 4 * HGRN_W + 2 * RET_KW + 2 * RET_VW
ROPE_BASE = 10000.0
N_EXPERTS = 32
TOP_K = 4
D_FF = 1024
SWIGLU_LIMIT = 7.0
SWIGLU_ALPHA = 1.702
EPS = 1e-6
CHUNK = 128
DEC_SEQ = 8
PAST_LEN = 16384
MOE_ROWS = 256
VMEM_LIMIT = 56 * 1024 * 1024

NT_DIMS = (((1,), (1,)), ((), ()))
TN_DIMS = (((0,), (0,)), ((), ()))


def _params(sem, vmem=VMEM_LIMIT):
    return pltpu.CompilerParams(dimension_semantics=sem, vmem_limit_bytes=vmem)


def _bdot(a, b):
    return jnp.dot(a.astype(BF16), b.astype(BF16), preferred_element_type=F32)


def _bdot_g(a, b, dims):
    return lax.dot_general(a.astype(BF16), b.astype(BF16), dims, preferred_element_type=F32)


def _sigmoid(x):
    return jax.nn.sigmoid(x)


def _rms(x):
    return x * lax.rsqrt(jnp.mean(x * x, axis=-1, keepdims=True) + EPS)


def _ada_kernel(c_ref, w_ref, b_ref, o_ref):
    c = c_ref[...]
    o_ref[...] = _bdot(c * _sigmoid(c), w_ref[...]) + b_ref[...]


def _ada(c_all, w, b, tn=512):
    L, _, N = w.shape
    R = c_all.shape[0]
    return pl.pallas_call(
        _ada_kernel,
        grid=(L, N // tn),
        in_specs=[
            pl.BlockSpec((R, D_MODEL), lambda l, n: (0, 0)),
            pl.BlockSpec((None, D_MODEL, tn), lambda l, n: (l, 0, n)),
            pl.BlockSpec((None, 1, tn), lambda l, n: (l, 0, n)),
        ],
        out_specs=pl.BlockSpec((None, R, tn), lambda l, n: (l, 0, n)),
        out_shape=jax.ShapeDtypeStruct((L, R, N), F32),
        compiler_params=_params(("arbitrary", "arbitrary")),
        name="ada",
    )(c_all, w, b.reshape(L, 1, N))


class _Tok:
    def __init__(self, n_prompt_seq, seq, n_sample_seq, tm):
        self.bp, self.seq, self.bs = n_prompt_seq, seq, n_sample_seq
        self.mp = n_prompt_seq * seq
        self.ms = n_sample_seq * DEC_SEQ
        self.m = self.mp + self.ms
        self.tm = tm
        assert self.mp % tm == 0 and self.ms % tm == 0 and seq % tm == 0
        self.tiles_p = self.mp // tm
        self.tiles_s = self.ms // tm
        self.tiles = self.tiles_p + self.tiles_s
        self.tiles_per_seq = seq // tm

    def modp_spec(self, layer, col):
        tps, last = self.tiles_per_seq, self.bp - 1
        return pl.BlockSpec((None, None, 1, D_MODEL),
                            lambda i: (layer, jnp.minimum(i // tps, last), 0, col))

    def mods_spec(self, layer, col):
        tp = self.tiles_p
        return pl.BlockSpec((None, self.tm, D_MODEL),
                            lambda i: (layer, jnp.maximum(i - tp, 0), col))

    def rows_spec(self, width, col=0):
        return pl.BlockSpec((self.tm, width), lambda i: (i, col))

    def split_specs(self, width):
        tp, ts = self.tiles_p, self.tiles_s
        return (pl.BlockSpec((self.tm, width), lambda i: (jnp.minimum(i, tp - 1), 0)),
                pl.BlockSpec((self.tm, width), lambda i: (jnp.clip(i - tp, 0, ts - 1), 0)))


def _pick(is_s, s_ref, p_ref):
    return jnp.where(is_s, s_ref[...], p_ref[...])


def _proj_kernel(xp_ref, xs_ref, psh, psc, ssh, ssc, g_ref, w_ref, o_ref, *, tiles_p):
    is_s = pl.program_id(0) >= tiles_p
    x = _pick(is_s, xs_ref, xp_ref)
    h = _rms(x) * g_ref[...] * (1.0 + _pick(is_s, ssc, psc)) + _pick(is_s, ssh, psh)
    o_ref[...] = jnp.dot(h.astype(BF16), w_ref[...], preferred_element_type=F32)


def _proj(tok, xp, xs, x_specs, modp, mods, layer, norm_g, w_in_bf16):
    return pl.pallas_call(
        functools.partial(_proj_kernel, tiles_p=tok.tiles_p),
        grid=(tok.tiles,),
        in_specs=[
            *x_specs,
            tok.modp_spec(layer, 0), tok.modp_spec(layer, 1),
            tok.mods_spec(layer, 0), tok.mods_spec(layer, 1),
            pl.BlockSpec((1, D_MODEL), lambda i: (0, 0)),
            pl.BlockSpec((D_MODEL, D_IN), lambda i: (0, 0)),
        ],
        out_specs=tok.rows_spec(D_IN),
        out_shape=jax.ShapeDtypeStruct((tok.m, D_IN), F32),
        compiler_params=_params(("arbitrary",)),
        name="proj",
    )(xp, xs, modp, modp, mods, mods, norm_g.reshape(1, D_MODEL), w_in_bf16)


def _block_ref_rows(b, blk, half):
    rows, width = b.shape
    if blk >= 8:
        b3 = b.reshape(rows // blk, blk, width)
        r = jnp.broadcast_to(b3[:, half - 1:half, :], b3.shape)
        return r.reshape(rows, width)
    b3 = b.reshape(rows // 8, 8, width)
    sub = lax.broadcasted_iota(jnp.int32, b3.shape, 1)
    r = jnp.broadcast_to(b3[:, half - 1:half, :], b3.shape)
    for j in range(1, 8 // blk):
        cand = jnp.broadcast_to(b3[:, j * blk + half - 1:j * blk + half, :], b3.shape)
        r = jnp.where(sub >= j * blk, cand, r)
    return r.reshape(rows, width)


def _lane_to_sublane(row_vec, eye):
    return jnp.sum(jnp.where(eye, jnp.broadcast_to(row_vec, eye.shape), 0.0), axis=-1, keepdims=True)


def _hgrn_head(qa, fa, ia, ga, lb, gn, states, seg):
    C = qa.shape[0]
    n_seg = C // seg
    seg_shift = int(math.log2(seg))
    q = qa * _sigmoid(qa)
    f = lb + (1.0 - lb) * _sigmoid(fa)
    logf = jnp.log(f)
    k = 1.0 - f
    v = ia
    v_b = v.astype(BF16)

    ri = lax.broadcasted_iota(jnp.int32, (C, C), 0)
    ci = lax.broadcasted_iota(jnp.int32, (C, C), 1)
    row = lax.broadcasted_iota(jnp.int32, (C, HGRN_DK), 0)
    causal = ri >= ci
    if n_seg > 1:
        causal = causal & ((ri >> seg_shift) == (ci >> seg_shift))
    b = jnp.dot(jnp.where(causal, 1.0, 0.0), logf, precision=HIGHEST,
                preferred_element_type=F32)

    att = jnp.zeros((C, C), F32)
    half = seg // 2
    while half >= 1:
        blk = 2 * half
        r = _block_ref_rows(b, blk, half)
        upper = (row & (blk - 1)) >= half
        e = jnp.exp(jnp.where(upper, b - r, r - b))
        qt = jnp.where(upper, q * e, 0.0)
        kt = jnp.where(upper, 0.0, k * e)
        a = _bdot_g(qt, kt, NT_DIMS)
        if blk < C:
            shift = int(math.log2(blk))
            a = jnp.where((ri >> shift) == (ci >> shift), a, 0.0)
        att = att + a
        half //= 2

    o = _bdot(att, v_b) + jnp.sum(q * k, axis=-1, keepdims=True) * v
    qe = q * jnp.exp(b)
    b_last = _block_ref_rows(b, seg, seg)
    kh = k * jnp.exp(b_last - b)
    eye = ri == ci
    rowseg = row >> seg_shift
    new_states = []
    for s in range(n_seg):
        if n_seg > 1:
            qe_s = jnp.where(rowseg == s, qe, 0.0)
            kh_s = jnp.where(rowseg == s, kh, 0.0)
        else:
            qe_s, kh_s = qe, kh
        o = o + _bdot(qe_s, states[s])
        decay = jnp.exp(_lane_to_sublane(b_last[s * seg:s * seg + 1, :], eye))
        new_states.append(decay * states[s] + _bdot_g(kh_s, v_b, TN_DIMS))
    on = _rms(o) * gn * _sigmoid(ga)
    return on, new_states


def _hgrn_prompt_kernel(q_ref, f_ref, i_ref, g_ref, lb_ref, gn_ref, o_ref, so_ref, s_scr):
    c = pl.program_id(1)

    @pl.when(c == 0)
    def _():
        s_scr[...] = jnp.zeros_like(s_scr)

    for h in range(N_HEADS):
        sl = slice(h * HGRN_DK, (h + 1) * HGRN_DK)
        on, new = _hgrn_head(q_ref[:, sl], f_ref[:, sl], i_ref[:, sl], g_ref[:, sl],
                             lb_ref[:, sl], gn_ref[:, sl], [s_scr[h]], CHUNK)
        o_ref[:, sl] = on.astype(BF16)
        s_scr[h] = new[0]

    @pl.when(c == pl.num_programs(1) - 1)
    def _():
        so_ref[...] = s_scr[...]


def _hgrn_sample_kernel(q_ref, f_ref, i_ref, g_ref, lb_ref, gn_ref, si_ref, o_ref, so_ref):
    n_seg = CHUNK // DEC_SEQ
    for h in range(N_HEADS):
        sl = slice(h * HGRN_DK, (h + 1) * HGRN_DK)
        on, new = _hgrn_head(q_ref[:, sl], f_ref[:, sl], i_ref[:, sl], g_ref[:, sl],
                             lb_ref[:, sl], gn_ref[:, sl],
                             [si_ref[s, h] for s in range(n_seg)], DEC_SEQ)
        o_ref[:, sl] = on.astype(BF16)
        for s in range(n_seg):
            so_ref[s, h] = new[s]


def _hgrn(tok, proj, lb, gn, state_in):
    lb2, gn2 = lb.reshape(1, HGRN_W), gn.reshape(1, HGRN_W)
    vec = lambda *_: (0, 0)
    chunks = tok.seq // CHUNK
    rows_p = lambda col: pl.BlockSpec((CHUNK, HGRN_W), lambda b, c: (b * chunks + c, col))
    oa_p, st_p = pl.pallas_call(
        _hgrn_prompt_kernel,
        grid=(tok.bp, chunks),
        in_specs=[rows_p(0), rows_p(1), rows_p(2), rows_p(3),
                  pl.BlockSpec((1, HGRN_W), vec), pl.BlockSpec((1, HGRN_W), vec)],
        out_specs=[pl.BlockSpec((CHUNK, HGRN_W), lambda b, c: (b * chunks + c, 0)),
                   pl.BlockSpec((None, N_HEADS, HGRN_DK, HGRN_DK), lambda b, c: (b, 0, 0, 0))],
        out_shape=[jax.ShapeDtypeStruct((tok.mp, HGRN_W), BF16),
                   jax.ShapeDtypeStruct((tok.bp, N_HEADS, HGRN_DK, HGRN_DK), F32)],
        scratch_shapes=[pltpu.VMEM((N_HEADS, HGRN_DK, HGRN_DK), F32)],
        compiler_params=_params(("arbitrary", "arbitrary")),
        name="hgrn_prompt",
    )(proj, proj, proj, proj, lb2, gn2)

    seqs = CHUNK // DEC_SEQ
    off = tok.mp // CHUNK
    rows_s = lambda col: pl.BlockSpec((CHUNK, HGRN_W), lambda i: (off + i, col))
    st_spec = pl.BlockSpec((seqs, N_HEADS, HGRN_DK, HGRN_DK), lambda i: (i, 0, 0, 0))
    oa_s, st_s = pl.pallas_call(
        _hgrn_sample_kernel,
        grid=(tok.ms // CHUNK,),
        in_specs=[rows_s(0), rows_s(1), rows_s(2), rows_s(3),
                  pl.BlockSpec((1, HGRN_W), vec), pl.BlockSpec((1, HGRN_W), vec), st_spec],
        out_specs=[pl.BlockSpec((CHUNK, HGRN_W), lambda i: (i, 0)), st_spec],
        out_shape=[jax.ShapeDtypeStruct((tok.ms, HGRN_W), BF16),
                   jax.ShapeDtypeStruct(state_in.shape, F32)],
        compiler_params=_params(("arbitrary",)),
        name="hgrn_sample",
    )(proj, proj, proj, proj, lb2, gn2, state_in)
    return oa_p, oa_s, st_p, st_s


LOG_GAMMA = tuple(math.log(1.0 - 2.0 ** (-5.0 - h)) for h in range(N_HEADS))


def _per_head(idx, values):
    out = jnp.full(idx.shape, values[-1], F32)
    for h in range(len(values) - 2, -1, -1):
        out = jnp.where(idx == h, values[h], out)
    return out


def _ret_core(qr, kr, vr, gr, cos, sin, gn, states, seg):
    C = qr.shape[0]
    n_seg = C // seg
    seg_shift = int(math.log2(seg))
    lane = lax.broadcasted_iota(jnp.int32, (C, RET_KW), 1)
    row = lax.broadcasted_iota(jnp.int32, (C, RET_KW), 0)
    lane_head = lane >> 6
    first_half = (lane & (RET_DK - 1)) < RET_DK // 2

    def rot(x):
        swapped = jnp.where(first_half, pltpu.roll(x, RET_KW - RET_DK // 2, 1),
                            pltpu.roll(x, RET_DK // 2, 1))
        return x * cos + swapped * sin

    q = rot(qr)
    k = rot(kr) * (RET_DK ** -0.5)
    k_b = k.astype(BF16)
    t = (row & (seg - 1)).astype(F32)
    lg_lane = _per_head(lane_head, LOG_GAMMA)
    q_dec = q * jnp.exp(lg_lane * (t + 1.0))
    k_dec = (k * jnp.exp(lg_lane * (seg - 1.0 - t))).astype(BF16)
    rowseg = row >> seg_shift

    ri = lax.broadcasted_iota(jnp.int32, (C, C), 0)
    ci = lax.broadcasted_iota(jnp.int32, (C, C), 1)
    causal = ri >= ci
    if n_seg > 1:
        causal = causal & ((ri >> seg_shift) == (ci >> seg_shift))
    rel = jnp.where(causal, (ri - ci).astype(F32), 0.0)

    outs = []
    for h in range(N_HEADS):
        vs = slice(h * RET_DV, (h + 1) * RET_DV)
        mh = lane_head == h
        att = _bdot_g(jnp.where(mh, q, 0.0), k_b, NT_DIMS)
        att = att * jnp.where(causal, jnp.exp(LOG_GAMMA[h] * rel), 0.0)
        o = _bdot(att, vr[:, vs])
        qd = jnp.where(mh, q_dec, 0.0)
        for s in range(n_seg):
            qd_s = jnp.where(rowseg == s, qd, 0.0) if n_seg > 1 else qd
            o = o + _bdot(qd_s, states[s])
        gate = gr[:, vs]
        outs.append(_rms(o) * gn[:, vs] * (gate * _sigmoid(gate)))

    srow = lax.broadcasted_iota(jnp.int32, (RET_KW, RET_DV), 0) >> 6
    s_decay = jnp.exp(_per_head(srow, LOG_GAMMA) * float(seg))
    v_b = vr.astype(BF16)
    new_states = []
    for s in range(n_seg):
        kd_s = jnp.where(rowseg == s, k_dec, jnp.zeros_like(k_dec)) if n_seg > 1 else k_dec
        u = lax.dot_general(kd_s, v_b, TN_DIMS, preferred_element_type=F32)
        upd = jnp.concatenate(
            [u[h * RET_DK:(h + 1) * RET_DK, h * RET_DV:(h + 1) * RET_DV] for h in range(N_HEADS)],
            axis=0)
        new_states.append(s_decay * states[s] + upd)
    return outs, new_states


def _ret_prompt_kernel(q_ref, k_ref, v_ref, g_ref, cos_ref, sin_ref, gn_ref, o_ref, so_ref, s_scr):
    c = pl.program_id(1)

    @pl.when(c == 0)
    def _():
        s_scr[...] = jnp.zeros_like(s_scr)

    outs, new = _ret_core(q_ref[...], k_ref[...], v_ref[...], g_ref[...], cos_ref[...], sin_ref[...],
                          gn_ref[...], [s_scr[...]], CHUNK)
    for h in range(N_HEADS):
        o_ref[:, h * RET_DV:(h + 1) * RET_DV] = outs[h].astype(BF16)
    s_scr[...] = new[0]

    @pl.when(c == pl.num_programs(1) - 1)
    def _():
        so_ref[...] = new[0].reshape(N_HEADS, RET_DK, RET_DV)


def _ret_sample_kernel(q_ref, k_ref, v_ref, g_ref, cos_ref, sin_ref, gn_ref, si_ref, o_ref, so_ref):
    n_seg = CHUNK // DEC_SEQ
    states = [si_ref[s].reshape(RET_KW, RET_DV) for s in range(n_seg)]
    outs, new = _ret_core(q_ref[...], k_ref[...], v_ref[...], g_ref[...], cos_ref[...], sin_ref[...],
                          gn_ref[...], states, DEC_SEQ)
    for h in range(N_HEADS):
        o_ref[:, h * RET_DV:(h + 1) * RET_DV] = outs[h].astype(BF16)
    for s in range(n_seg):
        so_ref[s] = new[s].reshape(N_HEADS, RET_DK, RET_DV)


def _rope_tables(pos):
    half = RET_DK // 2
    theta = 1.0 / (ROPE_BASE ** jnp.linspace(0.0, 1.0, half, dtype=F32))
    ang = pos[:, None] * theta[None, :]
    cos, sin = jnp.cos(ang), jnp.sin(ang)
    cos_t = jnp.tile(jnp.concatenate([cos, cos], axis=-1), (1, N_HEADS))
    sin_t = jnp.tile(jnp.concatenate([-sin, sin], axis=-1), (1, N_HEADS))
    return cos_t, sin_t


def _ret(tok, proj, gn, state_in):
    gn2 = gn.reshape(1, RET_VW)
    vec = lambda *_: (0, 0)
    chunks = tok.seq // CHUNK
    kcol = 4 * HGRN_W // RET_KW
    vcol = (4 * HGRN_W + 2 * RET_KW) // RET_VW
    cos_p, sin_p = _rope_tables(jnp.arange(tok.seq, dtype=F32))
    row_p = lambda b, c: b * chunks + c
    or_p, st_p = pl.pallas_call(
        _ret_prompt_kernel,
        grid=(tok.bp, chunks),
        in_specs=[pl.BlockSpec((CHUNK, RET_KW), lambda b, c: (row_p(b, c), kcol)),
                  pl.BlockSpec((CHUNK, RET_KW), lambda b, c: (row_p(b, c), kcol + 1)),
                  pl.BlockSpec((CHUNK, RET_VW), lambda b, c: (row_p(b, c), vcol)),
                  pl.BlockSpec((CHUNK, RET_VW), lambda b, c: (row_p(b, c), vcol + 1)),
                  pl.BlockSpec((CHUNK, RET_KW), lambda b, c: (c, 0)),
                  pl.BlockSpec((CHUNK, RET_KW), lambda b, c: (c, 0)),
                  pl.BlockSpec((1, RET_VW), vec)],
        out_specs=[pl.BlockSpec((CHUNK, RET_VW), lambda b, c: (row_p(b, c), 0)),
                   pl.BlockSpec((None, N_HEADS, RET_DK, RET_DV), lambda b, c: (b, 0, 0, 0))],
        out_shape=[jax.ShapeDtypeStruct((tok.mp, RET_VW), BF16),
                   jax.ShapeDtypeStruct((tok.bp, N_HEADS, RET_DK, RET_DV), F32)],
        scratch_shapes=[pltpu.VMEM((RET_KW, RET_DV), F32)],
        compiler_params=_params(("arbitrary", "arbitrary")),
        name="ret_prompt",
    )(proj, proj, proj, proj, cos_p, sin_p, gn2)

    seqs = CHUNK // DEC_SEQ
    off = tok.mp // CHUNK
    cos_s, sin_s = _rope_tables(PAST_LEN + jnp.arange(DEC_SEQ, dtype=F32))
    cos_s, sin_s = jnp.tile(cos_s, (seqs, 1)), jnp.tile(sin_s, (seqs, 1))
    st_spec = pl.BlockSpec((seqs, N_HEADS, RET_DK, RET_DV), lambda i: (i, 0, 0, 0))
    or_s, st_s = pl.pallas_call(
        _ret_sample_kernel,
        grid=(tok.ms // CHUNK,),
        in_specs=[pl.BlockSpec((CHUNK, RET_KW), lambda i: (off + i, kcol)),
                  pl.BlockSpec((CHUNK, RET_KW), lambda i: (off + i, kcol + 1)),
                  pl.BlockSpec((CHUNK, RET_VW), lambda i: (off + i, vcol)),
                  pl.BlockSpec((CHUNK, RET_VW), lambda i: (off + i, vcol + 1)),
                  pl.BlockSpec((CHUNK, RET_KW), vec),
                  pl.BlockSpec((CHUNK, RET_KW), vec),
                  pl.BlockSpec((1, RET_VW), vec), st_spec],
        out_specs=[pl.BlockSpec((CHUNK, RET_VW), lambda i: (i, 0)), st_spec],
        out_shape=[jax.ShapeDtypeStruct((tok.ms, RET_VW), BF16),
                   jax.ShapeDtypeStruct(state_in.shape, F32)],
        compiler_params=_params(("arbitrary",)),
        name="ret_sample",
    )(proj, proj, proj, proj, cos_s, sin_s, gn2, state_in)
    return or_p, or_s, st_p, st_s


def _post_kernel(xp_ref, xs_ref, oap, oas, orp, ors, wa_ref, wr_ref,
                 pg1, psh, psc, sg1, ssh, ssc, g_ref, rw_ref, rb_ref,
                 x1_ref, h2_ref, ti_ref, tg_ref, *, tiles_p):
    is_s = pl.program_id(0) >= tiles_p
    x = _pick(is_s, xs_ref, xp_ref)
    oa = _pick(is_s, oas, oap)
    orr = _pick(is_s, ors, orp)
    mix = (jnp.dot(oa, wa_ref[...], preferred_element_type=F32)
           + jnp.dot(orr, wr_ref[...], preferred_element_type=F32))
    x1 = x + _pick(is_s, sg1, pg1) * mix
    x1_ref[...] = x1
    h2 = _rms(x1) * g_ref[...] * (1.0 + _pick(is_s, ssc, psc)) + _pick(is_s, ssh, psh)
    for j in range(ROW_TILES):
        h2_ref[:, j, :] = h2[:, j * LANES:(j + 1) * LANES]

    logits = jnp.dot(h2, rw_ref[...], precision=HIGHEST, preferred_element_type=F32) + rb_ref[...]
    lane = lax.broadcasted_iota(jnp.int32, logits.shape, 1)
    lane_f = lane.astype(F32)
    neg = -jnp.inf
    l = jnp.where(lane < N_EXPERTS, logits, neg)
    vals, idxs = [], []
    for _ in range(TOP_K):
        m = jnp.max(l, axis=-1, keepdims=True)
        idx = jnp.min(jnp.where(l == m, lane_f, float(LANES)), axis=-1, keepdims=True)
        vals.append(m)
        idxs.append(idx)
        l = jnp.where(lane_f == idx, neg, l)
    es = [jnp.exp(v - vals[0]) for v in vals]
    inv = 1.0 / (es[0] + es[1] + es[2] + es[3])
    ti = jnp.zeros(logits.shape, F32)
    tg = jnp.zeros(logits.shape, F32)
    for kk in range(TOP_K):
        ti = jnp.where(lane == kk, idxs[kk], ti)
        tg = jnp.where(lane == kk, es[kk] * inv, tg)
    ti_ref[...] = ti.astype(jnp.int32)
    tg_ref[...] = tg


def _post(tok, xp, xs, x_specs, oa_p, oa_s, or_p, or_s, w_out_bf16, modp, mods, layer, norm_g,
          router_w_pad, router_b_pad):
    const = lambda i: (0, 0)
    return pl.pallas_call(
        functools.partial(_post_kernel, tiles_p=tok.tiles_p),
        grid=(tok.tiles,),
        in_specs=[
            *x_specs,
            *tok.split_specs(HGRN_W), *tok.split_specs(RET_VW),
            pl.BlockSpec((HGRN_W, D_MODEL), lambda i: (0, 0)),
            pl.BlockSpec((RET_VW, D_MODEL), lambda i: (1, 0)),
            tok.modp_spec(layer, 2), tok.modp_spec(layer, 3), tok.modp_spec(layer, 4),
            tok.mods_spec(layer, 2), tok.mods_spec(layer, 3), tok.mods_spec(layer, 4),
            pl.BlockSpec((1, D_MODEL), const),
            pl.BlockSpec((D_MODEL, LANES), const),
            pl.BlockSpec((1, LANES), const),
        ],
        out_specs=[tok.rows_spec(D_MODEL),
                   pl.BlockSpec((tok.tm, ROW_TILES, LANES), lambda i: (i, 0, 0)),
                   tok.rows_spec(LANES), tok.rows_spec(LANES)],
        out_shape=[jax.ShapeDtypeStruct((tok.m, D_MODEL), F32),
                   jax.ShapeDtypeStruct((tok.m, ROW_TILES, LANES), F32),
                   jax.ShapeDtypeStruct((tok.m, LANES), jnp.int32),
                   jax.ShapeDtypeStruct((tok.m, LANES), F32)],
        compiler_params=_params(("arbitrary",)),
        name="post",
    )(xp, xs, oa_p, oa_s, or_p, or_s, w_out_bf16, w_out_bf16,
      modp, modp, modp, mods, mods, mods, norm_g.reshape(1, D_MODEL), router_w_pad, router_b_pad)


def _start_row_gather(src_hbm, idx_ref, n_rows, dst, sem):
    def body(j, carry):
        pltpu.make_async_copy(src_hbm.at[idx_ref[0, j]], dst.at[j], sem).start()
        return carry

    lax.fori_loop(0, n_rows, body, 0)


def _wait_row_gather(src_hbm, n_rows, dst, sem):
    pltpu.make_async_copy(src_hbm.at[pl.ds(0, n_rows)], dst, sem).wait()


def _moe_kernel(be_ref, nu_ref, tok_cur, tok_nxt, h2_hbm, wgu_ref, bgu_ref, wdn_ref, bdn_ref, sg_ref,
                ys_ref, xbuf, sem, wgu_b, wdn_b):
    i = pl.program_id(0)
    n_used = nu_ref[0]
    slot = i % 2

    @pl.when(i == 0)
    def _():
        _start_row_gather(h2_hbm, tok_cur, MOE_ROWS, xbuf.at[0], sem.at[0])

    @pl.when(i + 1 < n_used)
    def _():
        _start_row_gather(h2_hbm, tok_nxt, MOE_ROWS, xbuf.at[1 - slot], sem.at[1 - slot])

    @pl.when(i < n_used)
    def _():
        @pl.when((i == 0) | (be_ref[i] != be_ref[jnp.maximum(i - 1, 0)]))
        def _():
            wgu_b[...] = wgu_ref[...].astype(BF16)
            wdn_b[...] = wdn_ref[...].astype(BF16)

        _wait_row_gather(h2_hbm, MOE_ROWS, xbuf.at[slot], sem.at[slot])
        x = jnp.concatenate([xbuf[slot, :, j, :] for j in range(ROW_TILES)], axis=-1).astype(BF16)
        gu = jnp.dot(x, wgu_b[...], preferred_element_type=F32) + bgu_ref[...]
        gate = jnp.minimum(gu[:, :D_FF], SWIGLU_LIMIT)
        up = jnp.clip(gu[:, D_FF:], -SWIGLU_LIMIT, SWIGLU_LIMIT)
        act = (up + 1.0) * (gate * _sigmoid(SWIGLU_ALPHA * gate))
        y = jnp.dot(act.astype(BF16), wdn_b[...], preferred_element_type=F32) + bdn_ref[...]
        y = y * sg_ref[...]
        for j in range(ROW_TILES):
            ys_ref[:, j, :] = y[:, j * LANES:(j + 1) * LANES]

    @pl.when(i >= n_used)
    def _():
        ys_ref[...] = jnp.zeros_like(ys_ref)


def _moe(h2g, block_expert, n_used, slot_tok, slot_gate, w_gu, b_gu, w_dn, b_dn, layer):
    n_blocks = block_expert.shape[0]
    tok3 = slot_tok.reshape(n_blocks, 1, MOE_ROWS)
    smem_tok = lambda f: pl.BlockSpec((None, 1, MOE_ROWS), f, memory_space=pltpu.SMEM)
    grid_spec = pltpu.PrefetchScalarGridSpec(
        num_scalar_prefetch=2,
        grid=(n_blocks,),
        in_specs=[
            smem_tok(lambda i, be, nu: (i, 0, 0)),
            smem_tok(lambda i, be, nu: (jnp.minimum(i + 1, n_blocks - 1), 0, 0)),
            pl.BlockSpec(memory_space=pl.ANY),
            pl.BlockSpec((None, None, D_MODEL, 2 * D_FF), lambda i, be, nu: (layer, be[i], 0, 0)),
            pl.BlockSpec((None, None, 1, 2 * D_FF), lambda i, be, nu: (layer, be[i], 0, 0)),
            pl.BlockSpec((None, None, D_FF, D_MODEL), lambda i, be, nu: (layer, be[i], 0, 0)),
            pl.BlockSpec((None, None, 1, D_MODEL), lambda i, be, nu: (layer, be[i], 0, 0)),
            pl.BlockSpec((MOE_ROWS, 1), lambda i, be, nu: (i, 0)),
        ],
        out_specs=pl.BlockSpec((MOE_ROWS, ROW_TILES, LANES), lambda i, be, nu: (i, 0, 0)),
        scratch_shapes=[
            pltpu.VMEM((2, MOE_ROWS, ROW_TILES, LANES), F32),
            pltpu.SemaphoreType.DMA((2,)),
            pltpu.VMEM((D_MODEL, 2 * D_FF), BF16),
            pltpu.VMEM((D_FF, D_MODEL), BF16),
        ],
    )
    depth = w_gu.shape[0]
    return pl.pallas_call(
        _moe_kernel,
        grid_spec=grid_spec,
        out_shape=jax.ShapeDtypeStruct((n_blocks * MOE_ROWS, ROW_TILES, LANES), F32),
        compiler_params=_params(("arbitrary",)),
        name="moe",
    )(block_expert, n_used, tok3, tok3, h2g, w_gu,
      b_gu.reshape(depth, N_EXPERTS, 1, 2 * D_FF), w_dn, b_dn.reshape(depth, N_EXPERTS, 1, D_MODEL),
      slot_gate.reshape(-1, 1))


def _route(top_i, top_g):
    m = top_i.shape[0]
    a = m * TOP_K
    flat_e = top_i.reshape(-1)
    flat_g = top_g.reshape(-1)
    counts = jnp.zeros((N_EXPERTS,), jnp.int32).at[flat_e].add(1)
    pad_counts = (counts + MOE_ROWS - 1) // MOE_ROWS * MOE_ROWS
    start = jnp.cumsum(counts) - counts
    pad_end = jnp.cumsum(pad_counts)
    pad_start = pad_end - pad_counts
    order = jnp.argsort(flat_e, stable=True).astype(jnp.int32)
    e_sorted = flat_e[order]
    dest = pad_start[e_sorted] + jnp.arange(a, dtype=jnp.int32) - start[e_sorted]
    n_blocks = (a + N_EXPERTS * (MOE_ROWS - 1) + MOE_ROWS - 1) // MOE_ROWS
    p = n_blocks * MOE_ROWS
    slot_tok = jnp.zeros((p,), jnp.int32).at[dest].set(order // TOP_K)
    slot_gate = jnp.zeros((p,), F32).at[dest].set(flat_g[order])
    slot_of = jnp.zeros((a,), jnp.int32).at[order].set(dest)
    n_used = (pad_end[-1] // MOE_ROWS).astype(jnp.int32)
    block_start = jnp.arange(n_blocks, dtype=jnp.int32) * MOE_ROWS
    block_expert = jnp.sum(block_start[:, None] >= pad_end[None, :], axis=1).astype(jnp.int32)
    last_expert = block_expert[jnp.maximum(n_used - 1, 0)]
    block_expert = jnp.where(jnp.arange(n_blocks) < n_used, block_expert, last_expert)
    return block_expert, n_used.reshape(1), slot_tok, slot_gate, slot_of


def _combine_kernel(sl_cur, sl_nxt, ys_hbm, x1_ref, pg2, sg2, x2_ref, buf, sem, *, tiles_p, tm):
    i = pl.program_id(0)
    n = pl.num_programs(0)
    slot = i % 2
    rows = TOP_K * tm

    @pl.when(i == 0)
    def _():
        _start_row_gather(ys_hbm, sl_cur, rows, buf.at[0], sem.at[0])

    @pl.when(i + 1 < n)
    def _():
        _start_row_gather(ys_hbm, sl_nxt, rows, buf.at[1 - slot], sem.at[1 - slot])

    _wait_row_gather(ys_hbm, rows, buf.at[slot], sem.at[slot])
    parts = []
    for j in range(ROW_TILES):
        acc = buf[slot, pl.ds(0, tm), j, :]
        for kk in range(1, TOP_K):
            acc = acc + buf[slot, pl.ds(kk * tm, tm), j, :]
        parts.append(acc)
    ff = jnp.concatenate(parts, axis=-1)
    is_s = i >= tiles_p
    x2_ref[...] = x1_ref[...] + _pick(is_s, sg2, pg2) * ff


def _combine(tok, ys, slot_of, x1, modp, mods, layer):
    tm = tok.tm
    sl3 = slot_of.reshape(tok.tiles, tm, TOP_K).transpose(0, 2, 1).reshape(tok.tiles, 1, TOP_K * tm)
    smem = lambda f: pl.BlockSpec((None, 1, TOP_K * tm), f, memory_space=pltpu.SMEM)
    last = tok.tiles - 1
    return pl.pallas_call(
        functools.partial(_combine_kernel, tiles_p=tok.tiles_p, tm=tm),
        grid=(tok.tiles,),
        in_specs=[smem(lambda i: (i, 0, 0)), smem(lambda i: (jnp.minimum(i + 1, last), 0, 0)),
                  pl.BlockSpec(memory_space=pl.ANY),
                  tok.rows_spec(D_MODEL),
                  tok.modp_spec(layer, 5), tok.mods_spec(layer, 5)],
        out_specs=tok.rows_spec(D_MODEL),
        out_shape=jax.ShapeDtypeStruct((tok.m, D_MODEL), F32),
        scratch_shapes=[pltpu.VMEM((2, TOP_K * tm, ROW_TILES, LANES), F32),
                        pltpu.SemaphoreType.DMA((2,))],
        compiler_params=_params(("arbitrary",)),
        name="combine",
    )(sl3, sl3, ys, x1, modp, mods)


def _final_kernel(x_ref, psh, psc, ssh, ssc, g_ref, o_ref, *, tiles_p):
    is_s = pl.program_id(0) >= tiles_p
    o_ref[...] = (_rms(x_ref[...]) * g_ref[...] * (1.0 + _pick(is_s, ssc, psc))
                  + _pick(is_s, ssh, psh))


def _final(tok, x, fmodp, fmods, final_g):
    return pl.pallas_call(
        functools.partial(_final_kernel, tiles_p=tok.tiles_p),
        grid=(tok.tiles,),
        in_specs=[tok.rows_spec(D_MODEL),
                  tok.modp_spec(0, 0), tok.modp_spec(0, 1),
                  tok.mods_spec(0, 0), tok.mods_spec(0, 1),
                  pl.BlockSpec((1, D_MODEL), lambda i: (0, 0))],
        out_specs=tok.rows_spec(D_MODEL),
        out_shape=jax.ShapeDtypeStruct((tok.m, D_MODEL), F32),
        compiler_params=_params(("arbitrary",)),
        name="final",
    )(x, fmodp, fmodp, fmods, fmods, final_g.reshape(1, D_MODEL))


def kernel(x_prompt, x_sample, c_prompt, c_sample, state_hgrn, state_ret, ada_w, ada_b, norm1_g, norm2_g, w_in, lb_logits, gn_hgrn, gn_ret, w_out, router_w, router_b, w_gate_up, b_gate_up, w_down, b_down, final_g, final_ada_w, final_ada_b):
    bp, seq, _ = x_prompt.shape
    bs = x_sample.shape[0]
    depth = ada_w.shape[0]
    assert x_sample.shape[1] == DEC_SEQ and seq % CHUNK == 0 and (bs * DEC_SEQ) % CHUNK == 0
    ms = bs * DEC_SEQ
    tm = 256 if (ms % 256 == 0 and seq % 256 == 0) else CHUNK
    tok = _Tok(bp, seq, bs, tm)

    c_all = jnp.concatenate([jnp.repeat(c_sample, DEC_SEQ, axis=0), c_prompt], axis=0)
    mod = _ada(c_all, ada_w, ada_b)
    fmod = _ada(c_all, final_ada_w[None], final_ada_b[None])
    modp = mod[:, ms:].reshape(depth, bp, 1, 6 * D_MODEL)
    fmodp = fmod[:, ms:].reshape(1, bp, 1, 2 * D_MODEL)

    lb_sm = jax.nn.softmax(lb_logits.astype(F32), axis=0)
    lb_all = jnp.cumsum(lb_sm, axis=0) - lb_sm[0]
    w_in_b = w_in.astype(BF16)
    w_out_b = w_out.astype(BF16)
    rw_pad = jnp.pad(router_w, ((0, 0), (0, 0), (0, LANES - N_EXPERTS)))
    rb_pad = jnp.pad(router_b, ((0, 0), (0, LANES - N_EXPERTS)))

    xp = x_prompt.reshape(tok.mp, D_MODEL)
    xs = x_sample.reshape(tok.ms, D_MODEL)
    x_specs = tok.split_specs(D_MODEL)
    new_a_p, new_a_s, new_r_p, new_r_s = [], [], [], []
    for l in range(depth):
        proj = _proj(tok, xp, xs, x_specs, modp, mod, l, norm1_g[l], w_in_b[l])
        oa_p, oa_s, sa_p, sa_s = _hgrn(tok, proj, lb_all[l], gn_hgrn[l], state_hgrn[l])
        or_p, or_s, sr_p, sr_s = _ret(tok, proj, gn_ret[l], state_ret[l])
        x1, h2g, top_i, top_g = _post(tok, xp, xs, x_specs, oa_p, oa_s, or_p, or_s, w_out_b[l],
                                      modp, mod, l, norm2_g[l], rw_pad[l], rb_pad[l].reshape(1, LANES))
        block_expert, n_used, slot_tok, slot_gate, slot_of = _route(top_i[:, :TOP_K], top_g[:, :TOP_K])
        ys = _moe(h2g, block_expert, n_used, slot_tok, slot_gate, w_gate_up, b_gate_up, w_down, b_down, l)
        x2 = _combine(tok, ys, slot_of, x1, modp, mod, l)
        xp = xs = x2
        tp = tok.tiles_p
        x_specs = (pl.BlockSpec((tm, D_MODEL), lambda i: (jnp.minimum(i, tp - 1), 0)),
                   pl.BlockSpec((tm, D_MODEL), lambda i: (jnp.maximum(i, tp), 0)))
        new_a_p.append(sa_p)
        new_a_s.append(sa_s)
        new_r_p.append(sr_p)
        new_r_s.append(sr_s)

    y = _final(tok, xp, fmodp, fmod, final_g)
    y_prompt = y[:tok.mp].reshape(bp, seq, D_MODEL)
    y_sample = y[tok.mp:].reshape(bs, DEC_SEQ, D_MODEL)
    return (y_prompt, y_sample, jnp.stack(new_a_p), jnp.stack(new_r_p),
            jnp.stack(new_a_s), jnp.stack(new_r_s))
```

```python
import functools
import math

import jax
import jax.numpy as jnp
from jax import lax
from jax.experimental import pallas as pl
from jax.experimental.pallas import tpu as pltpu

F32 = jnp.float32
BF16 = jnp.bfloat16
HIGHEST = lax.Precision.HIGHEST

D_MODEL = 1024
LANES = 128
ROW_TILES = D_MODEL // LANES
N_HEADS = 4
HGRN_DK = 128
HGRN_W = N_HEADS * HGRN_DK
RET_DK = 64
RET_DV = 128
RET_KW = N_HEADS * RET_DK
RET_VW = N_HEADS * RET_DV
D_IN = 4 * HGRN_W + 2 * RET_KW + 2 * RET_VW
ROPE_BASE = 10000.0
N_EXPERTS = 32
TOP_K = 4
D_FF = 1024
SWIGLU_LIMIT = 7.0
SWIGLU_ALPHA = 1.702
EPS = 1e-6
CHUNK = 128
DEC_SEQ = 8
PAST_LEN = 16384
MOE_ROWS = 256
VMEM_LIMIT = 56 * 1024 * 1024

NT_DIMS = (((1,), (1,)), ((), ()))
TN_DIMS = (((0,), (0,)), ((), ()))


def _params(sem, vmem=VMEM_LIMIT):
    return pltpu.CompilerParams(dimension_semantics=sem, vmem_limit_bytes=vmem)


def _bdot(a, b):
    return jnp.dot(a.astype(BF16), b.astype(BF16), preferred_element_type=F32)


def _bdot_g(a, b, dims):
    return lax.dot_general(a.astype(BF16), b.astype(BF16), dims, preferred_element_type=F32)


def _sigmoid(x):
    return jax.nn.sigmoid(x)


def _rms(x):
    return x * lax.rsqrt(jnp.mean(x * x, axis=-1, keepdims=True) + EPS)


def _ada_kernel(c_ref, w_ref, b_ref, o_ref):
    c = c_ref[...]
    o_ref[...] = _bdot(c * _sigmoid(c), w_ref[...]) + b_ref[...]


def _ada(c_all, w, b, tn=512):
    L, _, N = w.shape
    R = c_all.shape[0]
    return pl.pallas_call(
        _ada_kernel,
        grid=(L, N // tn),
        in_specs=[
            pl.BlockSpec((R, D_MODEL), lambda l, n: (0, 0)),
            pl.BlockSpec((None, D_MODEL, tn), lambda l, n: (l, 0, n)),
            pl.BlockSpec((None, 1, tn), lambda l, n: (l, 0, n)),
        ],
        out_specs=pl.BlockSpec((None, R, tn), lambda l, n: (l, 0, n)),
        out_shape=jax.ShapeDtypeStruct((L, R, N), F32),
        compiler_params=_params(("arbitrary", "arbitrary")),
        name="ada",
    )(c_all, w, b.reshape(L, 1, N))


class _Tok:
    def __init__(self, n_prompt_seq, seq, n_sample_seq, tm):
        self.bp, self.seq, self.bs = n_prompt_seq, seq, n_sample_seq
        self.mp = n_prompt_seq * seq
        self.ms = n_sample_seq * DEC_SEQ
        self.m = self.mp + self.ms
        self.tm = tm
        assert self.mp % tm == 0 and self.ms % tm == 0 and seq % tm == 0
        self.tiles_p = self.mp // tm
        self.tiles_s = self.ms // tm
        self.tiles = self.tiles_p + self.tiles_s
        self.tiles_per_seq = seq // tm

    def modp_spec(self, layer, col):
        tps, last = self.tiles_per_seq, self.bp - 1
        return pl.BlockSpec((None, None, 1, D_MODEL),
                            lambda i: (layer, jnp.minimum(i // tps, last), 0, col))

    def mods_spec(self, layer, col):
        tp = self.tiles_p
        return pl.BlockSpec((None, self.tm, D_MODEL),
                            lambda i: (layer, jnp.maximum(i - tp, 0), col))

    def rows_spec(self, width, col=0):
        return pl.BlockSpec((self.tm, width), lambda i: (i, col))

    def split_specs(self, width):
        tp, ts = self.tiles_p, self.tiles_s
        return (pl.BlockSpec((self.tm, width), lambda i: (jnp.minimum(i, tp - 1), 0)),
                pl.BlockSpec((self.tm, width), lambda i: (jnp.clip(i - tp, 0, ts - 1), 0)))


def _pick(is_s, s_ref, p_ref):
    return jnp.where(is_s, s_ref[...], p_ref[...])


def _proj_kernel(xp_ref, xs_ref, psh, psc, ssh, ssc, g_ref, w_ref, o_ref, *, tiles_p):
    is_s = pl.program_id(0) >= tiles_p
    x = _pick(is_s, xs_ref, xp_ref)
    h = _rms(x) * g_ref[...] * (1.0 + _pick(is_s, ssc, psc)) + _pick(is_s, ssh, psh)
    o_ref[...] = jnp.dot(h.astype(BF16), w_ref[...], preferred_element_type=F32)


def _proj(tok, xp, xs, x_specs, modp, mods, layer, norm_g, w_in_bf16):
    return pl.pallas_call(
        functools.partial(_proj_kernel, tiles_p=tok.tiles_p),
        grid=(tok.tiles,),
        in_specs=[
            *x_specs,
            tok.modp_spec(layer, 0), tok.modp_spec(layer, 1),
            tok.mods_spec(layer, 0), tok.mods_spec(layer, 1),
            pl.BlockSpec((1, D_MODEL), lambda i: (0, 0)),
            pl.BlockSpec((D_MODEL, D_IN), lambda i: (0, 0)),
        ],
        out_specs=tok.rows_spec(D_IN),
        out_shape=jax.ShapeDtypeStruct((tok.m, D_IN), F32),
        compiler_params=_params(("arbitrary",)),
        name="proj",
    )(xp, xs, modp, modp, mods, mods, norm_g.reshape(1, D_MODEL), w_in_bf16)


def _block_ref_rows(b, blk, half):
    rows, width = b.shape
    if blk >= 8:
        b3 = b.reshape(rows // blk, blk, width)
        r = jnp.broadcast_to(b3[:, half - 1:half, :], b3.shape)
        return r.reshape(rows, width)
    b3 = b.reshape(rows // 8, 8, width)
    sub = lax.broadcasted_iota(jnp.int32, b3.shape, 1)
    r = jnp.broadcast_to(b3[:, half - 1:half, :], b3.shape)
    for j in range(1, 8 // blk):
        cand = jnp.broadcast_to(b3[:, j * blk + half - 1:j * blk + half, :], b3.shape)
        r = jnp.where(sub >= j * blk, cand, r)
    return r.reshape(rows, width)


def _lane_to_sublane(row_vec, eye):
    return jnp.sum(jnp.where(eye, jnp.broadcast_to(row_vec, eye.shape), 0.0), axis=-1, keepdims=True)


def _hgrn_head(qa, fa, ia, ga, lb, gn, states, seg):
    C = qa.shape[0]
    n_seg = C // seg
    seg_shift = int(math.log2(seg))
    q = qa * _sigmoid(qa)
    f = lb + (1.0 - lb) * _sigmoid(fa)
    logf = jnp.log(f)
    k = 1.0 - f
    v = ia
    v_b = v.astype(BF16)

    ri = lax.broadcasted_iota(jnp.int32, (C, C), 0)
    ci = lax.broadcasted_iota(jnp.int32, (C, C), 1)
    row = lax.broadcasted_iota(jnp.int32, (C, HGRN_DK), 0)
    causal = ri >= ci
    if n_seg > 1:
        causal = causal & ((ri >> seg_shift) == (ci >> seg_shift))
    b = jnp.dot(jnp.where(causal, 1.0, 0.0), logf, precision=HIGHEST,
                preferred_element_type=F32)

    att = jnp.zeros((C, C), F32)
    half = seg // 2
    while half >= 1:
        blk = 2 * half
        r = _block_ref_rows(b, blk, half)
        upper = (row & (blk - 1)) >= half
        e = jnp.exp(jnp.where(upper, b - r, r - b))
        qt = jnp.where(upper, q * e, 0.0)
        kt = jnp.where(upper, 0.0, k * e)
        a = _bdot_g(qt, kt, NT_DIMS)
        if blk < C:
            shift = int(math.log2(blk))
            a = jnp.where((ri >> shift) == (ci >> shift), a, 0.0)
        att = att + a
        half //= 2

    o = _bdot(att, v_b) + jnp.sum(q * k, axis=-1, keepdims=True) * v
    qe = q * jnp.exp(b)
    b_last = _block_ref_rows(b, seg, seg)
    kh = k * jnp.exp(b_last - b)
    eye = ri == ci
    rowseg = row >> seg_shift
    new_states = []
    for s in range(n_seg):
        if n_seg > 1:
            qe_s = jnp.where(rowseg == s, qe, 0.0)
            kh_s = jnp.where(rowseg == s, kh, 0.0)
        else:
            qe_s, kh_s = qe, kh
        o = o + _bdot(qe_s, states[s])
        decay = jnp.exp(_lane_to_sublane(b_last[s * seg:s * seg + 1, :], eye))
        new_states.append(decay * states[s] + _bdot_g(kh_s, v_b, TN_DIMS))
    on = _rms(o) * gn * _sigmoid(ga)
    return on, new_states


def _hgrn_prompt_kernel(q_ref, f_ref, i_ref, g_ref, lb_ref, gn_ref, o_ref, so_ref, s_scr):
    c = pl.program_id(1)

    @pl.when(c == 0)
    def _():
        s_scr[...] = jnp.zeros_like(s_scr)

    for h in range(N_HEADS):
        sl = slice(h * HGRN_DK, (h + 1) * HGRN_DK)
        on, new = _hgrn_head(q_ref[:, sl], f_ref[:, sl], i_ref[:, sl], g_ref[:, sl],
                             lb_ref[:, sl], gn_ref[:, sl], [s_scr[h]], CHUNK)
        o_ref[:, sl] = on.astype(BF16)
        s_scr[h] = new[0]

    @pl.when(c == pl.num_programs(1) - 1)
    def _():
        so_ref[...] = s_scr[...]


def _hgrn_sample_kernel(q_ref, f_ref, i_ref, g_ref, lb_ref, gn_ref, si_ref, o_ref, so_ref):
    n_seg = CHUNK // DEC_SEQ
    for h in range(N_HEADS):
        sl = slice(h * HGRN_DK, (h + 1) * HGRN_DK)
        on, new = _hgrn_head(q_ref[:, sl], f_ref[:, sl], i_ref[:, sl], g_ref[:, sl],
                             lb_ref[:, sl], gn_ref[:, sl],
                             [si_ref[s, h] for s in range(n_seg)], DEC_SEQ)
        o_ref[:, sl] = on.astype(BF16)
        for s in range(n_seg):
            so_ref[s, h] = new[s]


def _hgrn(tok, proj, lb, gn, state_in):
    lb2, gn2 = lb.reshape(1, HGRN_W), gn.reshape(1, HGRN_W)
    vec = lambda *_: (0, 0)
    chunks = tok.seq // CHUNK
    rows_p = lambda col: pl.BlockSpec((CHUNK, HGRN_W), lambda b, c: (b * chunks + c, col))
    oa_p, st_p = pl.pallas_call(
        _hgrn_prompt_kernel,
        grid=(tok.bp, chunks),
        in_specs=[rows_p(0), rows_p(1), rows_p(2), rows_p(3),
                  pl.BlockSpec((1, HGRN_W), vec), pl.BlockSpec((1, HGRN_W), vec)],
        out_specs=[pl.BlockSpec((CHUNK, HGRN_W), lambda b, c: (b * chunks + c, 0)),
                   pl.BlockSpec((None, N_HEADS, HGRN_DK, HGRN_DK), lambda b, c: (b, 0, 0, 0))],
        out_shape=[jax.ShapeDtypeStruct((tok.mp, HGRN_W), BF16),
                   jax.ShapeDtypeStruct((tok.bp, N_HEADS, HGRN_DK, HGRN_DK), F32)],
        scratch_shapes=[pltpu.VMEM((N_HEADS, HGRN_DK, HGRN_DK), F32)],
        compiler_params=_params(("arbitrary", "arbitrary")),
        name="hgrn_prompt",
    )(proj, proj, proj, proj, lb2, gn2)

    seqs = CHUNK // DEC_SEQ
    off = tok.mp // CHUNK
    rows_s = lambda col: pl.BlockSpec((CHUNK, HGRN_W), lambda i: (off + i, col))
    st_spec = pl.BlockSpec((seqs, N_HEADS, HGRN_DK, HGRN_DK), lambda i: (i, 0, 0, 0))
    oa_s, st_s = pl.pallas_call(
        _hgrn_sample_kernel,
        grid=(tok.ms // CHUNK,),
        in_specs=[rows_s(0), rows_s(1), rows_s(2), rows_s(3),
                  pl.BlockSpec((1, HGRN_W), vec), pl.BlockSpec((1, HGRN_W), vec), st_spec],
        out_specs=[pl.BlockSpec((CHUNK, HGRN_W), lambda i: (i, 0)), st_spec],
        out_shape=[jax.ShapeDtypeStruct((tok.ms, HGRN_W), BF16),
                   jax.ShapeDtypeStruct(state_in.shape, F32)],
        compiler_params=_params(("arbitrary",)),
        name="hgrn_sample",
    )(proj, proj, proj, proj, lb2, gn2, state_in)
    return oa_p, oa_s, st_p, st_s


LOG_GAMMA = tuple(math.log(1.0 - 2.0 ** (-5.0 - h)) for h in range(N_HEADS))


def _per_head(idx, values):
    out = jnp.full(idx.shape, values[-1], F32)
    for h in range(len(values) - 2, -1, -1):
        out = jnp.where(idx == h, values[h], out)
    return out


def _ret_core(qr, kr, vr, gr, cos, sin, gn, states, seg):
    C = qr.shape[0]
    n_seg = C // seg
    seg_shift = int(math.log2(seg))
    lane = lax.broadcasted_iota(jnp.int32, (C, RET_KW), 1)
    row = lax.broadcasted_iota(jnp.int32, (C, RET_KW), 0)
    lane_head = lane >> 6
    first_half = (lane & (RET_DK - 1)) < RET_DK // 2

    def rot(x):
        swapped = jnp.where(first_half, pltpu.roll(x, RET_KW - RET_DK // 2, 1),
                            pltpu.roll(x, RET_DK // 2, 1))
        return x * cos + swapped * sin

    q = rot(qr)
    k = rot(kr) * (RET_DK ** -0.5)
    k_b = k.astype(BF16)
    t = (row & (seg - 1)).astype(F32)
    lg_lane = _per_head(lane_head, LOG_GAMMA)
    q_dec = q * jnp.exp(lg_lane * (t + 1.0))
    k_dec = (k * jnp.exp(lg_lane * (seg - 1.0 - t))).astype(BF16)
    rowseg = row >> seg_shift

    ri = lax.broadcasted_iota(jnp.int32, (C, C), 0)
    ci = lax.broadcasted_iota(jnp.int32, (C, C), 1)
    causal = ri >= ci
    if n_seg > 1:
        causal = causal & ((ri >> seg_shift) == (ci >> seg_shift))
    rel = jnp.where(causal, (ri - ci).astype(F32), 0.0)

    outs = []
    for h in range(N_HEADS):
        vs = slice(h * RET_DV, (h + 1) * RET_DV)
        mh = lane_head == h
        att = _bdot_g(jnp.where(mh, q, 0.0), k_b, NT_DIMS)
        att = att * jnp.where(causal, jnp.exp(LOG_GAMMA[h] * rel), 0.0)
        o = _bdot(att, vr[:, vs])
        qd = jnp.where(mh, q_dec, 0.0)
        for s in range(n_seg):
            qd_s = jnp.where(rowseg == s, qd, 0.0) if n_seg > 1 else qd
            o = o + _bdot(qd_s, states[s])
        gate = gr[:, vs]
        outs.append(_rms(o) * gn[:, vs] * (gate * _sigmoid(gate)))

    srow = lax.broadcasted_iota(jnp.int32, (RET_KW, RET_DV), 0) >> 6
    s_decay = jnp.exp(_per_head(srow, LOG_GAMMA) * float(seg))
    v_b = vr.astype(BF16)
    new_states = []
    for s in range(n_seg):
        kd_s = jnp.where(rowseg == s, k_dec, jnp.zeros_like(k_dec)) if n_seg > 1 else k_dec
        u = lax.dot_general(kd_s, v_b, TN_DIMS, preferred_element_type=F32)
        upd = jnp.concatenate(
            [u[h * RET_DK:(h + 1) * RET_DK, h * RET_DV:(h + 1) * RET_DV] for h in range(N_HEADS)],
            axis=0)
        new_states.append(s_decay * states[s] + upd)
    return outs, new_states


def _ret_prompt_kernel(q_ref, k_ref, v_ref, g_ref, cos_ref, sin_ref, gn_ref, o_ref, so_ref, s_scr):
    c = pl.program_id(1)

    @pl.when(c == 0)
    def _():
        s_scr[...] = jnp.zeros_like(s_scr)

    outs, new = _ret_core(q_ref[...], k_ref[...], v_ref[...], g_ref[...], cos_ref[...], sin_ref[...],
                          gn_ref[...], [s_scr[...]], CHUNK)
    for h in range(N_HEADS):
        o_ref[:, h * RET_DV:(h + 1) * RET_DV] = outs[h].astype(BF16)
    s_scr[...] = new[0]

    @pl.when(c == pl.num_programs(1) - 1)
    def _():
        so_ref[...] = new[0].reshape(N_HEADS, RET_DK, RET_DV)


def _ret_sample_kernel(q_ref, k_ref, v_ref, g_ref, cos_ref, sin_ref, gn_ref, si_ref, o_ref, so_ref):
    n_seg = CHUNK // DEC_SEQ
    states = [si_ref[s].reshape(RET_KW, RET_DV) for s in range(n_seg)]
    outs, new = _ret_core(q_ref[...], k_ref[...], v_ref[...], g_ref[...], cos_ref[...], sin_ref[...],
                          gn_ref[...], states, DEC_SEQ)
    for h in range(N_HEADS):
        o_ref[:, h * RET_DV:(h + 1) * RET_DV] = outs[h].astype(BF16)
    for s in range(n_seg):
        so_ref[s] = new[s].reshape(N_HEADS, RET_DK, RET_DV)


def _rope_tables(pos):
    half = RET_DK // 2
    theta = 1.0 / (ROPE_BASE ** jnp.linspace(0.0, 1.0, half, dtype=F32))
    ang = pos[:, None] * theta[None, :]
    cos, sin = jnp.cos(ang), jnp.sin(ang)
    cos_t = jnp.tile(jnp.concatenate([cos, cos], axis=-1), (1, N_HEADS))
    sin_t = jnp.tile(jnp.concatenate([-sin, sin], axis=-1), (1, N_HEADS))
    return cos_t, sin_t


def _ret(tok, proj, gn, state_in):
    gn2 = gn.reshape(1, RET_VW)
    vec = lambda *_: (0, 0)
    chunks = tok.seq // CHUNK
    kcol = 4 * HGRN_W // RET_KW
    vcol = (4 * HGRN_W + 2 * RET_KW) // RET_VW
    cos_p, sin_p = _rope_tables(jnp.arange(tok.seq, dtype=F32))
    row_p = lambda b, c: b * chunks + c
    or_p, st_p = pl.pallas_call(
        _ret_prompt_kernel,
        grid=(tok.bp, chunks),
        in_specs=[pl.BlockSpec((CHUNK, RET_KW), lambda b, c: (row_p(b, c), kcol)),
                  pl.BlockSpec((CHUNK, RET_KW), lambda b, c: (row_p(b, c), kcol + 1)),
                  pl.BlockSpec((CHUNK, RET_VW), lambda b, c: (row_p(b, c), vcol)),
                  pl.BlockSpec((CHUNK, RET_VW), lambda b, c: (row_p(b, c), vcol + 1)),
                  pl.BlockSpec((CHUNK, RET_KW), lambda b, c: (c, 0)),
                  pl.BlockSpec((CHUNK, RET_KW), lambda b, c: (c, 0)),
                  pl.BlockSpec((1, RET_VW), vec)],
        out_specs=[pl.BlockSpec((CHUNK, RET_VW), lambda b, c: (row_p(b, c), 0)),
                   pl.BlockSpec((None, N_HEADS, RET_DK, RET_DV), lambda b, c: (b, 0, 0, 0))],
        out_shape=[jax.ShapeDtypeStruct((tok.mp, RET_VW), BF16),
                   jax.ShapeDtypeStruct((tok.bp, N_HEADS, RET_DK, RET_DV), F32)],
        scratch_shapes=[pltpu.VMEM((RET_KW, RET_DV), F32)],
        compiler_params=_params(("arbitrary", "arbitrary")),
        name="ret_prompt",
    )(proj, proj, proj, proj, cos_p, sin_p, gn2)

    seqs = CHUNK // DEC_SEQ
    off = tok.mp // CHUNK
    cos_s, sin_s = _rope_tables(PAST_LEN + jnp.arange(DEC_SEQ, dtype=F32))
    cos_s, sin_s = jnp.tile(cos_s, (seqs, 1)), jnp.tile(sin_s, (seqs, 1))
    st_spec = pl.BlockSpec((seqs, N_HEADS, RET_DK, RET_DV), lambda i: (i, 0, 0, 0))
    or_s, st_s = pl.pallas_call(
        _ret_sample_kernel,
        grid=(tok.ms // CHUNK,),
        in_specs=[pl.BlockSpec((CHUNK, RET_KW), lambda i: (off + i, kcol)),
                  pl.BlockSpec((CHUNK, RET_KW), lambda i: (off + i, kcol + 1)),
                  pl.BlockSpec((CHUNK, RET_VW), lambda i: (off + i, vcol)),
                  pl.BlockSpec((CHUNK, RET_VW), lambda i: (off + i, vcol + 1)),
                  pl.BlockSpec((CHUNK, RET_KW), vec),
                  pl.BlockSpec((CHUNK, RET_KW), vec),
                  pl.BlockSpec((1, RET_VW), vec), st_spec],
        out_specs=[pl.BlockSpec((CHUNK, RET_VW), lambda i: (i, 0)), st_spec],
        out_shape=[jax.ShapeDtypeStruct((tok.ms, RET_VW), BF16),
                   jax.ShapeDtypeStruct(state_in.shape, F32)],
        compiler_params=_params(("arbitrary",)),
        name="ret_sample",
    )(proj, proj, proj, proj, cos_s, sin_s, gn2, state_in)
    return or_p, or_s, st_p, st_s


def _post_kernel(xp_ref, xs_ref, oap, oas, orp, ors, wa_ref, wr_ref,
                 pg1, psh, psc, sg1, ssh, ssc, g_ref, rw_ref, rb_ref,
                 x1_ref, h2_ref, meta_ref, cnt_ref, base_scr, *, tiles_p):
    @pl.when(pl.program_id(0) == 0)
    def _():
        base_scr[...] = jnp.zeros_like(base_scr)

    is_s = pl.program_id(0) >= tiles_p
    x = _pick(is_s, xs_ref, xp_ref)
    oa = _pick(is_s, oas, oap)
    orr = _pick(is_s, ors, orp)
    mix = (jnp.dot(oa, wa_ref[...], preferred_element_type=F32)
           + jnp.dot(orr, wr_ref[...], preferred_element_type=F32))
    x1 = x + _pick(is_s, sg1, pg1) * mix
    x1_ref[...] = x1
    h2 = _rms(x1) * g_ref[...] * (1.0 + _pick(is_s, ssc, psc)) + _pick(is_s, ssh, psh)
    h2_ref[...] = h2

    logits = jnp.dot(h2, rw_ref[...], precision=HIGHEST, preferred_element_type=F32) + rb_ref[...]
    lane = lax.broadcasted_iota(jnp.int32, logits.shape, 1)
    lane_f = lane.astype(F32)
    neg = -jnp.inf
    l = jnp.where(lane < N_EXPERTS, logits, neg)
    vals, idxs = [], []
    for _ in range(TOP_K):
        m = jnp.max(l, axis=-1, keepdims=True)
        idx = jnp.min(jnp.where(l == m, lane_f, float(LANES)), axis=-1, keepdims=True)
        vals.append(m)
        idxs.append(idx)
        l = jnp.where(lane_f == idx, neg, l)
    es = [jnp.exp(v - vals[0]) for v in vals]
    inv = 1.0 / (es[0] + es[1] + es[2] + es[3])

    tm = logits.shape[0]
    chosen = [lane_f == idx for idx in idxs]
    onehot = jnp.zeros(logits.shape, F32)
    for ch in chosen:
        onehot = jnp.where(ch, 1.0, onehot)
    ri = lax.broadcasted_iota(jnp.int32, (tm, tm), 0)
    ci = lax.broadcasted_iota(jnp.int32, (tm, tm), 1)
    earlier = _bdot(jnp.where(ri > ci, 1.0, 0.0), onehot)
    rank = earlier + base_scr[...]
    base_scr[...] += jnp.sum(onehot, axis=0, keepdims=True)
    cnt_ref[...] = base_scr[...]

    meta = jnp.zeros(logits.shape, F32)
    for kk in range(TOP_K):
        rk = jnp.sum(jnp.where(chosen[kk], rank, 0.0), axis=-1, keepdims=True)
        meta = jnp.where(lane == kk, idxs[kk], meta)
        meta = jnp.where(lane == TOP_K + kk, es[kk] * inv, meta)
        meta = jnp.where(lane == 2 * TOP_K + kk, rk, meta)
    meta_ref[...] = meta


def _post(tok, xp, xs, x_specs, oa_p, oa_s, or_p, or_s, w_out_bf16, modp, mods, layer, norm_g,
          router_w_pad, router_b_pad):
    const = lambda i: (0, 0)
    return pl.pallas_call(
        functools.partial(_post_kernel, tiles_p=tok.tiles_p),
        grid=(tok.tiles,),
        in_specs=[
            *x_specs,
            *tok.split_specs(HGRN_W), *tok.split_specs(RET_VW),
            pl.BlockSpec((HGRN_W, D_MODEL), lambda i: (0, 0)),
            pl.BlockSpec((RET_VW, D_MODEL), lambda i: (1, 0)),
            tok.modp_spec(layer, 2), tok.modp_spec(layer, 3), tok.modp_spec(layer, 4),
            tok.mods_spec(layer, 2), tok.mods_spec(layer, 3), tok.mods_spec(layer, 4),
            pl.BlockSpec((1, D_MODEL), const),
            pl.BlockSpec((D_MODEL, LANES), const),
            pl.BlockSpec((1, LANES), const),
        ],
        out_specs=[tok.rows_spec(D_MODEL), tok.rows_spec(D_MODEL), tok.rows_spec(LANES),
                   pl.BlockSpec((1, LANES), const)],
        out_shape=[jax.ShapeDtypeStruct((tok.m, D_MODEL), F32),
                   jax.ShapeDtypeStruct((tok.m, D_MODEL), F32),
                   jax.ShapeDtypeStruct((tok.m, LANES), F32),
                   jax.ShapeDtypeStruct((1, LANES), F32)],
        scratch_shapes=[pltpu.VMEM((1, LANES), F32)],
        compiler_params=_params(("arbitrary",)),
        name="post",
    )(xp, xs, oa_p, oa_s, or_p, or_s, w_out_bf16, w_out_bf16,
      modp, modp, modp, mods, mods, mods, norm_g.reshape(1, D_MODEL), router_w_pad, router_b_pad)


GATHER_UNROLL = 8
FF_CHUNKS = 4


def _start_row_gather(src_hbm, idx_at, n_rows, dst, sem, unroll):
    def one(r):
        pltpu.make_async_copy(src_hbm.at[pl.ds(idx_at(r), 1), :], dst.at[pl.ds(r, 1), :], sem).start()

    if unroll >= n_rows:
        for r in range(n_rows):
            one(r)
        return

    def body(g, carry):
        for u in range(unroll):
            one(g * unroll + u)
        return carry

    lax.fori_loop(0, n_rows // unroll, body, 0)


def _wait_row_gather(src_hbm, n_rows, dst, sem):
    pltpu.make_async_copy(src_hbm.at[pl.ds(0, n_rows), :], dst, sem).wait()


def _moe_kernel(be_ref, cs_ref, nu_ref, toks_ref, h2_hbm, wgu_ref, bgu_ref, wdn_ref, bdn_ref,
                ys_ref, xbuf, sem, xb, act, wgu_b, wdn_b):
    i = pl.program_id(0)
    n_used = nu_ref[0]
    slot = i % 2

    def gather(block, dst_slot, unroll):
        base = cs_ref[block]
        _start_row_gather(h2_hbm, lambda r: toks_ref[base + r], MOE_ROWS, xbuf.at[dst_slot],
                          sem.at[dst_slot], unroll)

    @pl.when(i == 0)
    def _():
        gather(0, 0, GATHER_UNROLL)

    @pl.when(i < n_used)
    def _():
        @pl.when((i == 0) | (be_ref[i] != be_ref[jnp.maximum(i - 1, 0)]))
        def _():
            wgu_b[...] = wgu_ref[...].astype(BF16)
            wdn_b[...] = wdn_ref[...].astype(BF16)

        _wait_row_gather(h2_hbm, MOE_ROWS, xbuf.at[slot], sem.at[slot])
        xb[...] = xbuf[slot].astype(BF16)
        base = cs_ref[i + 1]
        rows_per_chunk = MOE_ROWS // FF_CHUNKS
        cw = D_FF // FF_CHUNKS
        for c in range(FF_CHUNKS):
            _start_row_gather(h2_hbm, lambda r: toks_ref[base + c * rows_per_chunk + r], rows_per_chunk,
                              xbuf.at[1 - slot, pl.ds(c * rows_per_chunk, rows_per_chunk)],
                              sem.at[1 - slot], rows_per_chunk)
            gsl = slice(c * cw, (c + 1) * cw)
            usl = slice(D_FF + c * cw, D_FF + (c + 1) * cw)
            gate = jnp.dot(xb[...], wgu_b[:, gsl], preferred_element_type=F32) + bgu_ref[:, gsl]
            up = jnp.dot(xb[...], wgu_b[:, usl], preferred_element_type=F32) + bgu_ref[:, usl]
            gate = jnp.minimum(gate, SWIGLU_LIMIT)
            up = jnp.clip(up, -SWIGLU_LIMIT, SWIGLU_LIMIT)
            act[:, gsl] = ((up + 1.0) * (gate * _sigmoid(SWIGLU_ALPHA * gate))).astype(BF16)
        ys_ref[...] = jnp.dot(act[...], wdn_b[...], preferred_element_type=F32) + bdn_ref[...]

    @pl.when(i >= n_used)
    def _():
        @pl.when(i == n_used)
        def _():
            _wait_row_gather(h2_hbm, MOE_ROWS, xbuf.at[slot], sem.at[slot])

        ys_ref[...] = jnp.zeros_like(ys_ref)


def _moe(h2, block_expert, block_cstart, n_used, toks, w_gu, b_gu, w_dn, b_dn, layer):
    n_blocks = block_expert.shape[0]
    idx = lambda i, be, cs, nu, tk: (layer, be[i], 0, 0)
    grid_spec = pltpu.PrefetchScalarGridSpec(
        num_scalar_prefetch=4,
        grid=(n_blocks,),
        in_specs=[
            pl.BlockSpec(memory_space=pl.ANY),
            pl.BlockSpec((None, None, D_MODEL, 2 * D_FF), idx),
            pl.BlockSpec((None, None, 1, 2 * D_FF), idx),
            pl.BlockSpec((None, None, D_FF, D_MODEL), idx),
            pl.BlockSpec((None, None, 1, D_MODEL), idx),
        ],
        out_specs=pl.BlockSpec((MOE_ROWS, D_MODEL), lambda i, be, cs, nu, tk: (i, 0)),
        scratch_shapes=[
            pltpu.VMEM((2, MOE_ROWS, D_MODEL), F32),
            pltpu.SemaphoreType.DMA((2,)),
            pltpu.VMEM((MOE_ROWS, D_MODEL), BF16),
            pltpu.VMEM((MOE_ROWS, D_FF), BF16),
            pltpu.VMEM((D_MODEL, 2 * D_FF), BF16),
            pltpu.VMEM((D_FF, D_MODEL), BF16),
        ],
    )
    depth = w_gu.shape[0]
    return pl.pallas_call(
        _moe_kernel,
        grid_spec=grid_spec,
        out_shape=jax.ShapeDtypeStruct((n_blocks * MOE_ROWS, D_MODEL), F32),
        compiler_params=_params(("arbitrary",)),
        name="moe",
    )(block_expert, block_cstart, n_used, toks, h2, w_gu,
      b_gu.reshape(depth, N_EXPERTS, 1, 2 * D_FF), w_dn, b_dn.reshape(depth, N_EXPERTS, 1, D_MODEL))


def _route(meta, counts_f):
    m = meta.shape[0]
    a = m * TOP_K
    counts = counts_f[0, :N_EXPERTS].astype(jnp.int32)
    expert = meta[:, 0:TOP_K].astype(jnp.int32)
    rank = meta[:, 2 * TOP_K:3 * TOP_K].astype(jnp.int32)
    pad_counts = (counts + MOE_ROWS - 1) // MOE_ROWS * MOE_ROWS
    start = jnp.cumsum(counts) - counts
    pad_end = jnp.cumsum(pad_counts)
    pad_start = pad_end - pad_counts
    slot_of = jnp.take(pad_start, expert) + rank
    _, toks = lax.sort((slot_of.reshape(-1), jnp.arange(a, dtype=jnp.int32) // TOP_K), num_keys=1)
    toks = jnp.concatenate([toks, jnp.zeros((MOE_ROWS,), jnp.int32)])
    n_blocks = (a + N_EXPERTS * (MOE_ROWS - 1) + MOE_ROWS - 1) // MOE_ROWS + 1
    n_used = (pad_end[-1] // MOE_ROWS).astype(jnp.int32)
    block_start = jnp.arange(n_blocks, dtype=jnp.int32) * MOE_ROWS
    used = jnp.arange(n_blocks) < n_used
    block_expert = jnp.minimum(jnp.sum(block_start[:, None] >= pad_end[None, :], axis=1),
                               N_EXPERTS - 1).astype(jnp.int32)
    block_cstart = jnp.where(used, start[block_expert] + block_start - pad_start[block_expert], 0)
    last_expert = block_expert[jnp.maximum(n_used - 1, 0)]
    block_expert = jnp.where(used, block_expert, last_expert)
    return block_expert, block_cstart.astype(jnp.int32), n_used.reshape(1), toks, slot_of


def _combine_kernel(sl_cur, sl_nxt, ys_hbm, x1_ref, meta_ref, pg2, sg2, x2_ref, buf, sem, *, tiles_p, tm):
    i = pl.program_id(0)
    n = pl.num_programs(0)
    slot = i % 2
    rows = TOP_K * tm

    @pl.when(i == 0)
    def _():
        _start_row_gather(ys_hbm, lambda r: sl_cur[0, r], rows, buf.at[0], sem.at[0], GATHER_UNROLL)

    @pl.when(i + 1 < n)
    def _():
        _start_row_gather(ys_hbm, lambda r: sl_nxt[0, r], rows, buf.at[1 - slot], sem.at[1 - slot],
                          GATHER_UNROLL)

    _wait_row_gather(ys_hbm, rows, buf.at[slot], sem.at[slot])
    meta = meta_ref[...]
    ff = meta[:, TOP_K:TOP_K + 1] * buf[slot, pl.ds(0, tm), :]
    for kk in range(1, TOP_K):
        ff = ff + meta[:, TOP_K + kk:TOP_K + kk + 1] * buf[slot, pl.ds(kk * tm, tm), :]
    is_s = i >= tiles_p
    x2_ref[...] = x1_ref[...] + _pick(is_s, sg2, pg2) * ff


def _combine(tok, ys, slot_of, x1, meta, modp, mods, layer):
    tm = tok.tm
    sl3 = slot_of.reshape(tok.tiles, tm, TOP_K).transpose(0, 2, 1).reshape(tok.tiles, 1, TOP_K * tm)
    smem = lambda f: pl.BlockSpec((None, 1, TOP_K * tm), f, memory_space=pltpu.SMEM)
    last = tok.tiles - 1
    return pl.pallas_call(
        functools.partial(_combine_kernel, tiles_p=tok.tiles_p, tm=tm),
        grid=(tok.tiles,),
        in_specs=[smem(lambda i: (i, 0, 0)), smem(lambda i: (jnp.minimum(i + 1, last), 0, 0)),
                  pl.BlockSpec(memory_space=pl.ANY),
                  tok.rows_spec(D_MODEL), tok.rows_spec(LANES),
                  tok.modp_spec(layer, 5), tok.mods_spec(layer, 5)],
        out_specs=tok.rows_spec(D_MODEL),
        out_shape=jax.ShapeDtypeStruct((tok.m, D_MODEL), F32),
        scratch_shapes=[pltpu.VMEM((2, TOP_K * tm, D_MODEL), F32),
                        pltpu.SemaphoreType.DMA((2,))],
        compiler_params=_params(("arbitrary",)),
        name="combine",
    )(sl3, sl3, ys, x1, meta, modp, mods)


def _final_kernel(x_ref, psh, psc, ssh, ssc, g_ref, o_ref, *, tiles_p):
    is_s = pl.program_id(0) >= tiles_p
    o_ref[...] = (_rms(x_ref[...]) * g_ref[...] * (1.0 + _pick(is_s, ssc, psc))
                  + _pick(is_s, ssh, psh))


def _final(tok, x, fmodp, fmods, final_g):
    return pl.pallas_call(
        functools.partial(_final_kernel, tiles_p=tok.tiles_p),
        grid=(tok.tiles,),
        in_specs=[tok.rows_spec(D_MODEL),
                  tok.modp_spec(0, 0), tok.modp_spec(0, 1),
                  tok.mods_spec(0, 0), tok.mods_spec(0, 1),
                  pl.BlockSpec((1, D_MODEL), lambda i: (0, 0))],
        out_specs=tok.rows_spec(D_MODEL),
        out_shape=jax.ShapeDtypeStruct((tok.m, D_MODEL), F32),
        compiler_params=_params(("arbitrary",)),
        name="final",
    )(x, fmodp, fmodp, fmods, fmods, final_g.reshape(1, D_MODEL))


def kernel(x_prompt, x_sample, c_prompt, c_sample, state_hgrn, state_ret, ada_w, ada_b, norm1_g, norm2_g, w_in, lb_logits, gn_hgrn, gn_ret, w_out, router_w, router_b, w_gate_up, b_gate_up, w_down, b_down, final_g, final_ada_w, final_ada_b):
    bp, seq, _ = x_prompt.shape
    bs = x_sample.shape[0]
    depth = ada_w.shape[0]
    assert x_sample.shape[1] == DEC_SEQ and seq % CHUNK == 0 and (bs * DEC_SEQ) % CHUNK == 0
    ms = bs * DEC_SEQ
    tm = 256 if (ms % 256 == 0 and seq % 256 == 0) else CHUNK
    tok = _Tok(bp, seq, bs, tm)

    c_all = jnp.concatenate([jnp.repeat(c_sample, DEC_SEQ, axis=0), c_prompt], axis=0)
    mod = _ada(c_all, ada_w, ada_b)
    fmod = _ada(c_all, final_ada_w[None], final_ada_b[None])
    modp = mod[:, ms:].reshape(depth, bp, 1, 6 * D_MODEL)
    fmodp = fmod[:, ms:].reshape(1, bp, 1, 2 * D_MODEL)

    lb_sm = jax.nn.softmax(lb_logits.astype(F32), axis=0)
    lb_all = jnp.cumsum(lb_sm, axis=0) - lb_sm[0]
    w_in_b = w_in.astype(BF16)
    w_out_b = w_out.astype(BF16)
    rw_pad = jnp.pad(router_w, ((0, 0), (0, 0), (0, LANES - N_EXPERTS)))
    rb_pad = jnp.pad(router_b, ((0, 0), (0, LANES - N_EXPERTS)))

    xp = x_prompt.reshape(tok.mp, D_MODEL)
    xs = x_sample.reshape(tok.ms, D_MODEL)
    x_specs = tok.split_specs(D_MODEL)
    new_a_p, new_a_s, new_r_p, new_r_s = [], [], [], []
    for l in range(depth):
        proj = _proj(tok, xp, xs, x_specs, modp, mod, l, norm1_g[l], w_in_b[l])
        oa_p, oa_s, sa_p, sa_s = _hgrn(tok, proj, lb_all[l], gn_hgrn[l], state_hgrn[l])
        or_p, or_s, sr_p, sr_s = _ret(tok, proj, gn_ret[l], state_ret[l])
        x1, h2, meta, counts = _post(tok, xp, xs, x_specs, oa_p, oa_s, or_p, or_s, w_out_b[l],
                                     modp, mod, l, norm2_g[l], rw_pad[l], rb_pad[l].reshape(1, LANES))
        block_expert, block_cstart, n_used, toks, slot_of = _route(meta, counts)
        ys = _moe(h2, block_expert, block_cstart, n_used, toks, w_gate_up, b_gate_up, w_down, b_down, l)
        x2 = _combine(tok, ys, slot_of, x1, meta, modp, mod, l)
        xp = xs = x2
        tp = tok.tiles_p
        x_specs = (pl.BlockSpec((tm, D_MODEL), lambda i: (jnp.minimum(i, tp - 1), 0)),
                   pl.BlockSpec((tm, D_MODEL), lambda i: (jnp.maximum(i, tp), 0)))
        new_a_p.append(sa_p)
        new_a_s.append(sa_s)
        new_r_p.append(sr_p)
        new_r_s.append(sr_s)

    y = _final(tok, xp, fmodp, fmod, final_g)
    y_prompt = y[:tok.mp].reshape(bp, seq, D_MODEL)
    y_sample = y[tok.mp:].reshape(bs, DEC_SEQ, D_MODEL)
    return (y_prompt, y_sample, jnp.stack(new_a_p), jnp.stack(new_r_p),
            jnp.stack(new_a_s), jnp.stack(new_r_s))
```

```python
import functools
import math

import jax
import jax.numpy as jnp
import numpy as np
from jax import lax
from jax.experimental import pallas as pl
from jax.experimental.pallas import tpu as pltpu

F32 = jnp.float32
BF16 = jnp.bfloat16
HIGHEST = lax.Precision.HIGHEST

D_MODEL = 1024
LANES = 128
ROW_TILES = D_MODEL // LANES
N_HEADS = 4
HGRN_DK = 128
HGRN_W = N_HEADS * HGRN_DK
RET_DK = 64
RET_DV = 128
RET_KW = N_HEADS * RET_DK
RET_VW = N_HEADS * RET_DV
D_IN = 4 * HGRN_W + 2 * RET_KW + 2 * RET_VW
ROPE_BASE = 10000.0
N_EXPERTS = 32
TOP_K = 4
D_FF = 1024
SWIGLU_LIMIT = 7.0
SWIGLU_ALPHA = 1.702
EPS = 1e-6
LOG2_E = 1.0 / math.log(2.0)
CHUNK = 128
DEC_SEQ = 8
PAST_LEN = 16384
MOE_ROWS = 256
VMEM_LIMIT = 56 * 1024 * 1024

NT_DIMS = (((1,), (1,)), ((), ()))
TN_DIMS = (((0,), (0,)), ((), ()))


def _params(sem, vmem=VMEM_LIMIT):
    return pltpu.CompilerParams(dimension_semantics=sem, vmem_limit_bytes=vmem)


def _bdot(a, b):
    return jnp.dot(a.astype(BF16), b.astype(BF16), preferred_element_type=F32)


def _bdot_g(a, b, dims):
    return lax.dot_general(a.astype(BF16), b.astype(BF16), dims, preferred_element_type=F32)


def _sigmoid(x):
    return jax.nn.sigmoid(x)


def _rms(x):
    return x * lax.rsqrt(jnp.mean(x * x, axis=-1, keepdims=True) + EPS)


def _ada_kernel(c_ref, w_ref, b_ref, o_ref):
    c = c_ref[...]
    o_ref[...] = _bdot(c * _sigmoid(c), w_ref[...]) + b_ref[...]


def _ada(c_all, w, b, tn=512):
    L, _, N = w.shape
    R = c_all.shape[0]
    return pl.pallas_call(
        _ada_kernel,
        grid=(L, N // tn),
        in_specs=[
            pl.BlockSpec((R, D_MODEL), lambda l, n: (0, 0)),
            pl.BlockSpec((None, D_MODEL, tn), lambda l, n: (l, 0, n)),
            pl.BlockSpec((None, 1, tn), lambda l, n: (l, 0, n)),
        ],
        out_specs=pl.BlockSpec((None, R, tn), lambda l, n: (l, 0, n)),
        out_shape=jax.ShapeDtypeStruct((L, R, N), F32),
        compiler_params=_params(("arbitrary", "arbitrary")),
        name="ada",
    )(c_all, w, b.reshape(L, 1, N))


class _Tok:
    def __init__(self, n_prompt_seq, seq, n_sample_seq, tm):
        self.bp, self.seq, self.bs = n_prompt_seq, seq, n_sample_seq
        self.mp = n_prompt_seq * seq
        self.ms = n_sample_seq * DEC_SEQ
        self.m = self.mp + self.ms
        self.tm = tm
        assert self.mp % tm == 0 and self.ms % tm == 0 and seq % tm == 0
        self.tiles_p = self.mp // tm
        self.tiles_s = self.ms // tm
        self.tiles = self.tiles_p + self.tiles_s
        self.tiles_per_seq = seq // tm

    def modp_spec(self, layer, col):
        tps, last = self.tiles_per_seq, self.bp - 1
        return pl.BlockSpec((None, None, 1, D_MODEL),
                            lambda i: (layer, jnp.minimum(i // tps, last), 0, col))

    def mods_spec(self, layer, col):
        tp = self.tiles_p
        return pl.BlockSpec((None, self.tm, D_MODEL),
                            lambda i: (layer, jnp.maximum(i - tp, 0), col))

    def rows_spec(self, width, col=0):
        return pl.BlockSpec((self.tm, width), lambda i: (i, col))

    def split_specs(self, width):
        tp, ts = self.tiles_p, self.tiles_s
        return (pl.BlockSpec((self.tm, width), lambda i: (jnp.minimum(i, tp - 1), 0)),
                pl.BlockSpec((self.tm, width), lambda i: (jnp.clip(i - tp, 0, ts - 1), 0)))


def _pick(is_s, s_ref, p_ref):
    return jnp.where(is_s, s_ref[...], p_ref[...])


def _proj_kernel(xp_ref, xs_ref, psh, psc, ssh, ssc, g_ref, w_ref, o_ref, *, tiles_p):
    is_s = pl.program_id(0) >= tiles_p
    x = _pick(is_s, xs_ref, xp_ref)
    h = _rms(x) * g_ref[...] * (1.0 + _pick(is_s, ssc, psc)) + _pick(is_s, ssh, psh)
    o_ref[...] = jnp.dot(h.astype(BF16), w_ref[...], preferred_element_type=F32)


def _proj(tok, xp, xs, x_specs, modp, mods, layer, norm_g, w_in_bf16):
    return pl.pallas_call(
        functools.partial(_proj_kernel, tiles_p=tok.tiles_p),
        grid=(tok.tiles,),
        in_specs=[
            *x_specs,
            tok.modp_spec(layer, 0), tok.modp_spec(layer, 1),
            tok.mods_spec(layer, 0), tok.mods_spec(layer, 1),
            pl.BlockSpec((1, D_MODEL), lambda i: (0, 0)),
            pl.BlockSpec((D_MODEL, D_IN), lambda i: (0, 0)),
        ],
        out_specs=tok.rows_spec(D_IN),
        out_shape=jax.ShapeDtypeStruct((tok.m, D_IN), F32),
        compiler_params=_params(("arbitrary",)),
        name="proj",
    )(xp, xs, modp, modp, mods, mods, norm_g.reshape(1, D_MODEL), w_in_bf16)


def _block_ref_rows(b, blk, half):
    rows, width = b.shape
    if blk >= 8:
        b3 = b.reshape(rows // blk, blk, width)
        r = jnp.broadcast_to(b3[:, half - 1:half, :], b3.shape)
        return r.reshape(rows, width)
    b3 = b.reshape(rows // 8, 8, width)
    sub = lax.broadcasted_iota(jnp.int32, b3.shape, 1)
    r = jnp.broadcast_to(b3[:, half - 1:half, :], b3.shape)
    for j in range(1, 8 // blk):
        cand = jnp.broadcast_to(b3[:, j * blk + half - 1:j * blk + half, :], b3.shape)
        r = jnp.where(sub >= j * blk, cand, r)
    return r.reshape(rows, width)


def _hgrn_consts(rows, seg):
    t = np.arange(rows)[:, None]
    s = np.arange(rows)[None, :]
    mats = [(t >= s) & (t // seg == s // seg), t == s]
    half = seg // 2
    while half >= 1:
        blk = 2 * half
        mats.append((t // blk == s // blk) & (t % blk >= half) & (s % blk < half))
        half //= 2
    return jnp.asarray(np.stack(mats).astype(np.float32))


def _lane_to_sublane(row_vec, eye):
    return jnp.sum(eye * row_vec, axis=-1, keepdims=True)


def _hgrn_core(qa, fa, ia, ga, lb, gn, state, seg, c_ref, o_ref, store_state):
    C = qa.shape[0]
    n_seg = C // seg
    heads = [slice(h * HGRN_DK, (h + 1) * HGRN_DK) for h in range(N_HEADS)]
    q = qa * _sigmoid(qa)
    f = lb + (1.0 - lb) * _sigmoid(fa)
    k = 1.0 - f
    v = ia
    v_b = v.astype(BF16)
    b = jnp.dot(c_ref[0], jnp.log(f) * LOG2_E, precision=HIGHEST, preferred_element_type=F32)

    att = [None] * N_HEADS
    half, level = seg // 2, 2
    while half >= 1:
        e = jnp.exp2(-jnp.abs(b - _block_ref_rows(b, 2 * half, half)))
        qe = (q * e).astype(BF16)
        ke = (k * e).astype(BF16)
        for h, sl in enumerate(heads):
            a = lax.dot_general(qe[:, sl], ke[:, sl], NT_DIMS, preferred_element_type=F32) * c_ref[level]
            att[h] = a if att[h] is None else att[h] + a
        half //= 2
        level += 1

    qk = q * k
    qb = (q * jnp.exp2(b)).astype(BF16)
    b_last = _block_ref_rows(b, seg, seg)
    kb = (k * jnp.exp2(b_last - b)).astype(BF16)
    eye = c_ref[1]
    if n_seg > 1:
        rowseg = lax.broadcasted_iota(jnp.int32, (C, HGRN_DK), 0) >> int(math.log2(seg))
    zero = jnp.zeros((C, HGRN_DK), BF16)
    for h, sl in enumerate(heads):
        o = _bdot(att[h], v_b[:, sl]) + jnp.sum(qk[:, sl], axis=-1, keepdims=True) * v[:, sl]
        for s in range(n_seg):
            qb_s = jnp.where(rowseg == s, qb[:, sl], zero) if n_seg > 1 else qb[:, sl]
            kb_s = jnp.where(rowseg == s, kb[:, sl], zero) if n_seg > 1 else kb[:, sl]
            st = state(s, h)
            o = o + _bdot(qb_s, st)
            decay = jnp.exp2(_lane_to_sublane(b_last[s * seg:s * seg + 1, sl], eye))
            store_state(s, h, decay * st + lax.dot_general(kb_s, v_b[:, sl], TN_DIMS,
                                                           preferred_element_type=F32))
        o_ref[:, sl] = (_rms(o) * gn[:, sl] * _sigmoid(ga[:, sl])).astype(BF16)


def _hgrn_prompt_kernel(q_ref, f_ref, i_ref, g_ref, lb_ref, gn_ref, c_ref, o_ref, so_ref, s_scr):
    c = pl.program_id(1)

    @pl.when(c == 0)
    def _():
        s_scr[...] = jnp.zeros_like(s_scr)

    def store(s, h, val):
        s_scr[h] = val

    _hgrn_core(q_ref[...], f_ref[...], i_ref[...], g_ref[...], lb_ref[...], gn_ref[...],
               lambda s, h: s_scr[h], CHUNK, c_ref, o_ref, store)

    @pl.when(c == pl.num_programs(1) - 1)
    def _():
        so_ref[...] = s_scr[...]


def _hgrn_sample_kernel(q_ref, f_ref, i_ref, g_ref, lb_ref, gn_ref, c_ref, si_ref, o_ref, so_ref):
    def store(s, h, val):
        so_ref[s, h] = val

    _hgrn_core(q_ref[...], f_ref[...], i_ref[...], g_ref[...], lb_ref[...], gn_ref[...],
               lambda s, h: si_ref[s, h], DEC_SEQ, c_ref, o_ref, store)


def _hgrn(tok, proj, lb, gn, state_in):
    lb2, gn2 = lb.reshape(1, HGRN_W), gn.reshape(1, HGRN_W)
    vec = lambda *_: (0, 0)
    chunks = tok.seq // CHUNK
    consts_p = _hgrn_consts(CHUNK, CHUNK)
    rows_p = lambda col: pl.BlockSpec((CHUNK, HGRN_W), lambda b, c: (b * chunks + c, col))
    oa_p, st_p = pl.pallas_call(
        _hgrn_prompt_kernel,
        grid=(tok.bp, chunks),
        in_specs=[rows_p(0), rows_p(1), rows_p(2), rows_p(3),
                  pl.BlockSpec((1, HGRN_W), vec), pl.BlockSpec((1, HGRN_W), vec),
                  pl.BlockSpec(consts_p.shape, lambda b, c: (0, 0, 0))],
        out_specs=[pl.BlockSpec((CHUNK, HGRN_W), lambda b, c: (b * chunks + c, 0)),
                   pl.BlockSpec((None, N_HEADS, HGRN_DK, HGRN_DK), lambda b, c: (b, 0, 0, 0))],
        out_shape=[jax.ShapeDtypeStruct((tok.mp, HGRN_W), BF16),
                   jax.ShapeDtypeStruct((tok.bp, N_HEADS, HGRN_DK, HGRN_DK), F32)],
        scratch_shapes=[pltpu.VMEM((N_HEADS, HGRN_DK, HGRN_DK), F32)],
        compiler_params=_params(("arbitrary", "arbitrary")),
        name="hgrn_prompt",
    )(proj, proj, proj, proj, lb2, gn2, consts_p)

    seqs = CHUNK // DEC_SEQ
    off = tok.mp // CHUNK
    consts_s = _hgrn_consts(CHUNK, DEC_SEQ)
    rows_s = lambda col: pl.BlockSpec((CHUNK, HGRN_W), lambda i: (off + i, col))
    st_spec = pl.BlockSpec((seqs, N_HEADS, HGRN_DK, HGRN_DK), lambda i: (i, 0, 0, 0))
    oa_s, st_s = pl.pallas_call(
        _hgrn_sample_kernel,
        grid=(tok.ms // CHUNK,),
        in_specs=[rows_s(0), rows_s(1), rows_s(2), rows_s(3),
                  pl.BlockSpec((1, HGRN_W), vec), pl.BlockSpec((1, HGRN_W), vec),
                  pl.BlockSpec(consts_s.shape, lambda i: (0, 0, 0)), st_spec],
        out_specs=[pl.BlockSpec((CHUNK, HGRN_W), lambda i: (i, 0)), st_spec],
        out_shape=[jax.ShapeDtypeStruct((tok.ms, HGRN_W), BF16),
                   jax.ShapeDtypeStruct(state_in.shape, F32)],
        compiler_params=_params(("arbitrary",)),
        name="hgrn_sample",
    )(proj, proj, proj, proj, lb2, gn2, consts_s, state_in)
    return oa_p, oa_s, st_p, st_s


LOG_GAMMA = tuple(math.log(1.0 - 2.0 ** (-5.0 - h)) for h in range(N_HEADS))


def _per_head(idx, values):
    out = jnp.full(idx.shape, values[-1], F32)
    for h in range(len(values) - 2, -1, -1):
        out = jnp.where(idx == h, values[h], out)
    return out


def _ret_core(qr, kr, vr, gr, cos, sin, gn, states, seg):
    C = qr.shape[0]
    n_seg = C // seg
    seg_shift = int(math.log2(seg))
    lane = lax.broadcasted_iota(jnp.int32, (C, RET_KW), 1)
    row = lax.broadcasted_iota(jnp.int32, (C, RET_KW), 0)
    lane_head = lane >> 6
    first_half = (lane & (RET_DK - 1)) < RET_DK // 2

    def rot(x):
        swapped = jnp.where(first_half, pltpu.roll(x, RET_KW - RET_DK // 2, 1),
                            pltpu.roll(x, RET_DK // 2, 1))
        return x * cos + swapped * sin

    q = rot(qr)
    k = rot(kr) * (RET_DK ** -0.5)
    k_b = k.astype(BF16)
    t = (row & (seg - 1)).astype(F32)
    lg_lane = _per_head(lane_head, LOG_GAMMA)
    q_dec = q * jnp.exp(lg_lane * (t + 1.0))
    k_dec = (k * jnp.exp(lg_lane * (seg - 1.0 - t))).astype(BF16)
    rowseg = row >> seg_shift

    ri = lax.broadcasted_iota(jnp.int32, (C, C), 0)
    ci = lax.broadcasted_iota(jnp.int32, (C, C), 1)
    causal = ri >= ci
    if n_seg > 1:
        causal = causal & ((ri >> seg_shift) == (ci >> seg_shift))
    rel = jnp.where(causal, (ri - ci).astype(F32), 0.0)

    outs = []
    for h in range(N_HEADS):
        vs = slice(h * RET_DV, (h + 1) * RET_DV)
        mh = lane_head == h
        att = _bdot_g(jnp.where(mh, q, 0.0), k_b, NT_DIMS)
        att = att * jnp.where(causal, jnp.exp(LOG_GAMMA[h] * rel), 0.0)
        o = _bdot(att, vr[:, vs])
        qd = jnp.where(mh, q_dec, 0.0)
        for s in range(n_seg):
            qd_s = jnp.where(rowseg == s, qd, 0.0) if n_seg > 1 else qd
            o = o + _bdot(qd_s, states[s])
        gate = gr[:, vs]
        outs.append(_rms(o) * gn[:, vs] * (gate * _sigmoid(gate)))

    srow = lax.broadcasted_iota(jnp.int32, (RET_KW, RET_DV), 0) >> 6
    s_decay = jnp.exp(_per_head(srow, LOG_GAMMA) * float(seg))
    v_b = vr.astype(BF16)
    new_states = []
    for s in range(n_seg):
        kd_s = jnp.where(rowseg == s, k_dec, jnp.zeros_like(k_dec)) if n_seg > 1 else k_dec
        u = lax.dot_general(kd_s, v_b, TN_DIMS, preferred_element_type=F32)
        upd = jnp.concatenate(
            [u[h * RET_DK:(h + 1) * RET_DK, h * RET_DV:(h + 1) * RET_DV] for h in range(N_HEADS)],
            axis=0)
        new_states.append(s_decay * states[s] + upd)
    return outs, new_states


def _ret_prompt_kernel(q_ref, k_ref, v_ref, g_ref, cos_ref, sin_ref, gn_ref, o_ref, so_ref, s_scr):
    c = pl.program_id(1)

    @pl.when(c == 0)
    def _():
        s_scr[...] = jnp.zeros_like(s_scr)

    outs, new = _ret_core(q_ref[...], k_ref[...], v_ref[...], g_ref[...], cos_ref[...], sin_ref[...],
                          gn_ref[...], [s_scr[...]], CHUNK)
    for h in range(N_HEADS):
        o_ref[:, h * RET_DV:(h + 1) * RET_DV] = outs[h].astype(BF16)
    s_scr[...] = new[0]

    @pl.when(c == pl.num_programs(1) - 1)
    def _():
        so_ref[...] = new[0].reshape(N_HEADS, RET_DK, RET_DV)


def _ret_sample_kernel(q_ref, k_ref, v_ref, g_ref, cos_ref, sin_ref, gn_ref, si_ref, o_ref, so_ref):
    n_seg = CHUNK // DEC_SEQ
    states = [si_ref[s].reshape(RET_KW, RET_DV) for s in range(n_seg)]
    outs, new = _ret_core(q_ref[...], k_ref[...], v_ref[...], g_ref[...], cos_ref[...], sin_ref[...],
                          gn_ref[...], states, DEC_SEQ)
    for h in range(N_HEADS):
        o_ref[:, h * RET_DV:(h + 1) * RET_DV] = outs[h].astype(BF16)
    for s in range(n_seg):
        so_ref[s] = new[s].reshape(N_HEADS, RET_DK, RET_DV)


def _rope_tables(pos):
    half = RET_DK // 2
    theta = 1.0 / (ROPE_BASE ** jnp.linspace(0.0, 1.0, half, dtype=F32))
    ang = pos[:, None] * theta[None, :]
    cos, sin = jnp.cos(ang), jnp.sin(ang)
    cos_t = jnp.tile(jnp.concatenate([cos, cos], axis=-1), (1, N_HEADS))
    sin_t = jnp.tile(jnp.concatenate([-sin, sin], axis=-1), (1, N_HEADS))
    return cos_t, sin_t


def _ret(tok, proj, gn, state_in):
    gn2 = gn.reshape(1, RET_VW)
    vec = lambda *_: (0, 0)
    chunks = tok.seq // CHUNK
    kcol = 4 * HGRN_W // RET_KW
    vcol = (4 * HGRN_W + 2 * RET_KW) // RET_VW
    cos_p, sin_p = _rope_tables(jnp.arange(tok.seq, dtype=F32))
    row_p = lambda b, c: b * chunks + c
    or_p, st_p = pl.pallas_call(
        _ret_prompt_kernel,
        grid=(tok.bp, chunks),
        in_specs=[pl.BlockSpec((CHUNK, RET_KW), lambda b, c: (row_p(b, c), kcol)),
                  pl.BlockSpec((CHUNK, RET_KW), lambda b, c: (row_p(b, c), kcol + 1)),
                  pl.BlockSpec((CHUNK, RET_VW), lambda b, c: (row_p(b, c), vcol)),
                  pl.BlockSpec((CHUNK, RET_VW), lambda b, c: (row_p(b, c), vcol + 1)),
                  pl.BlockSpec((CHUNK, RET_KW), lambda b, c: (c, 0)),
                  pl.BlockSpec((CHUNK, RET_KW), lambda b, c: (c, 0)),
                  pl.BlockSpec((1, RET_VW), vec)],
        out_specs=[pl.BlockSpec((CHUNK, RET_VW), lambda b, c: (row_p(b, c), 0)),
                   pl.BlockSpec((None, N_HEADS, RET_DK, RET_DV), lambda b, c: (b, 0, 0, 0))],
        out_shape=[jax.ShapeDtypeStruct((tok.mp, RET_VW), BF16),
                   jax.ShapeDtypeStruct((tok.bp, N_HEADS, RET_DK, RET_DV), F32)],
        scratch_shapes=[pltpu.VMEM((RET_KW, RET_DV), F32)],
        compiler_params=_params(("arbitrary", "arbitrary")),
        name="ret_prompt",
    )(proj, proj, proj, proj, cos_p, sin_p, gn2)

    seqs = CHUNK // DEC_SEQ
    off = tok.mp // CHUNK
    cos_s, sin_s = _rope_tables(PAST_LEN + jnp.arange(DEC_SEQ, dtype=F32))
    cos_s, sin_s = jnp.tile(cos_s, (seqs, 1)), jnp.tile(sin_s, (seqs, 1))
    st_spec = pl.BlockSpec((seqs, N_HEADS, RET_DK, RET_DV), lambda i: (i, 0, 0, 0))
    or_s, st_s = pl.pallas_call(
        _ret_sample_kernel,
        grid=(tok.ms // CHUNK,),
        in_specs=[pl.BlockSpec((CHUNK, RET_KW), lambda i: (off + i, kcol)),
                  pl.BlockSpec((CHUNK, RET_KW), lambda i: (off + i, kcol + 1)),
                  pl.BlockSpec((CHUNK, RET_VW), lambda i: (off + i, vcol)),
                  pl.BlockSpec((CHUNK, RET_VW), lambda i: (off + i, vcol + 1)),
                  pl.BlockSpec((CHUNK, RET_KW), vec),
                  pl.BlockSpec((CHUNK, RET_KW), vec),
                  pl.BlockSpec((1, RET_VW), vec), st_spec],
        out_specs=[pl.BlockSpec((CHUNK, RET_VW), lambda i: (i, 0)), st_spec],
        out_shape=[jax.ShapeDtypeStruct((tok.ms, RET_VW), BF16),
                   jax.ShapeDtypeStruct(state_in.shape, F32)],
        compiler_params=_params(("arbitrary",)),
        name="ret_sample",
    )(proj, proj, proj, proj, cos_s, sin_s, gn2, state_in)
    return or_p, or_s, st_p, st_s


def _post_kernel(xp_ref, xs_ref, oap, oas, orp, ors, wa_ref, wr_ref,
                 pg1, psh, psc, sg1, ssh, ssc, g_ref, rw_ref, rb_ref,
                 x1_ref, h2_ref, meta_ref, cnt_ref, base_scr, *, tiles_p):
    @pl.when(pl.program_id(0) == 0)
    def _():
        base_scr[...] = jnp.zeros_like(base_scr)

    is_s = pl.program_id(0) >= tiles_p
    x = _pick(is_s, xs_ref, xp_ref)
    oa = _pick(is_s, oas, oap)
    orr = _pick(is_s, ors, orp)
    mix = (jnp.dot(oa, wa_ref[...], preferred_element_type=F32)
           + jnp.dot(orr, wr_ref[...], preferred_element_type=F32))
    x1 = x + _pick(is_s, sg1, pg1) * mix
    x1_ref[...] = x1
    h2 = _rms(x1) * g_ref[...] * (1.0 + _pick(is_s, ssc, psc)) + _pick(is_s, ssh, psh)
    h2_ref[...] = h2

    logits = jnp.dot(h2, rw_ref[...], precision=HIGHEST, preferred_element_type=F32) + rb_ref[...]
    lane = lax.broadcasted_iota(jnp.int32, logits.shape, 1)
    lane_f = lane.astype(F32)
    neg = -jnp.inf
    l = jnp.where(lane < N_EXPERTS, logits, neg)
    vals, idxs = [], []
    for _ in range(TOP_K):
        m = jnp.max(l, axis=-1, keepdims=True)
        idx = jnp.min(jnp.where(l == m, lane_f, float(LANES)), axis=-1, keepdims=True)
        vals.append(m)
        idxs.append(idx)
        l = jnp.where(lane_f == idx, neg, l)
    es = [jnp.exp(v - vals[0]) for v in vals]
    inv = 1.0 / (es[0] + es[1] + es[2] + es[3])

    tm = logits.shape[0]
    chosen = [lane_f == idx for idx in idxs]
    onehot = jnp.zeros(logits.shape, F32)
    for ch in chosen:
        onehot = jnp.where(ch, 1.0, onehot)
    ri = lax.broadcasted_iota(jnp.int32, (tm, tm), 0)
    ci = lax.broadcasted_iota(jnp.int32, (tm, tm), 1)
    earlier = _bdot(jnp.where(ri > ci, 1.0, 0.0), onehot)
    rank = earlier + base_scr[...]
    base_scr[...] += jnp.sum(onehot, axis=0, keepdims=True)
    cnt_ref[...] = base_scr[...]

    meta = jnp.zeros(logits.shape, F32)
    for kk in range(TOP_K):
        rk = jnp.sum(jnp.where(chosen[kk], rank, 0.0), axis=-1, keepdims=True)
        meta = jnp.where(lane == kk, idxs[kk], meta)
        meta = jnp.where(lane == TOP_K + kk, es[kk] * inv, meta)
        meta = jnp.where(lane == 2 * TOP_K + kk, rk, meta)
    meta_ref[...] = meta


def _post(tok, xp, xs, x_specs, oa_p, oa_s, or_p, or_s, w_out_bf16, modp, mods, layer, norm_g,
          router_w_pad, router_b_pad):
    const = lambda i: (0, 0)
    return pl.pallas_call(
        functools.partial(_post_kernel, tiles_p=tok.tiles_p),
        grid=(tok.tiles,),
        in_specs=[
            *x_specs,
            *tok.split_specs(HGRN_W), *tok.split_specs(RET_VW),
            pl.BlockSpec((HGRN_W, D_MODEL), lambda i: (0, 0)),
            pl.BlockSpec((RET_VW, D_MODEL), lambda i: (1, 0)),
            tok.modp_spec(layer, 2), tok.modp_spec(layer, 3), tok.modp_spec(layer, 4),
            tok.mods_spec(layer, 2), tok.mods_spec(layer, 3), tok.mods_spec(layer, 4),
            pl.BlockSpec((1, D_MODEL), const),
            pl.BlockSpec((D_MODEL, LANES), const),
            pl.BlockSpec((1, LANES), const),
        ],
        out_specs=[tok.rows_spec(D_MODEL), tok.rows_spec(D_MODEL), tok.rows_spec(LANES),
                   pl.BlockSpec((1, LANES), const)],
        out_shape=[jax.ShapeDtypeStruct((tok.m, D_MODEL), F32),
                   jax.ShapeDtypeStruct((tok.m, D_MODEL), F32),
                   jax.ShapeDtypeStruct((tok.m, LANES), F32),
                   jax.ShapeDtypeStruct((1, LANES), F32)],
        scratch_shapes=[pltpu.VMEM((1, LANES), F32)],
        compiler_params=_params(("arbitrary",)),
        name="post",
    )(xp, xs, oa_p, oa_s, or_p, or_s, w_out_bf16, w_out_bf16,
      modp, modp, modp, mods, mods, mods, norm_g.reshape(1, D_MODEL), router_w_pad, router_b_pad)


GATHER_UNROLL = 8
FF_CHUNKS = 4


def _start_row_gather(src_hbm, idx_at, n_rows, dst, sem, unroll):
    def one(r):
        pltpu.make_async_copy(src_hbm.at[pl.ds(idx_at(r), 1), :], dst.at[pl.ds(r, 1), :], sem).start()

    if unroll >= n_rows:
        for r in range(n_rows):
            one(r)
        return

    def body(g, carry):
        for u in range(unroll):
            one(g * unroll + u)
        return carry

    lax.fori_loop(0, n_rows // unroll, body, 0)


def _wait_row_gather(src_hbm, n_rows, dst, sem):
    pltpu.make_async_copy(src_hbm.at[pl.ds(0, n_rows), :], dst, sem).wait()


def _moe_kernel(be_ref, cs_ref, nu_ref, toks_ref, h2_hbm, wgu_ref, bgu_ref, wdn_ref, bdn_ref,
                ys_ref, xbuf, sem, xb, act, wgu_b, wdn_b):
    i = pl.program_id(0)
    n_used = nu_ref[0]
    slot = i % 2

    def gather(block, dst_slot, unroll):
        base = cs_ref[block]
        _start_row_gather(h2_hbm, lambda r: toks_ref[base + r], MOE_ROWS, xbuf.at[dst_slot],
                          sem.at[dst_slot], unroll)

    @pl.when(i == 0)
    def _():
        gather(0, 0, GATHER_UNROLL)

    @pl.when(i + 1 < n_used)
    def _():
        gather(i + 1, 1 - slot, GATHER_UNROLL)

    @pl.when(i < n_used)
    def _():
        @pl.when((i == 0) | (be_ref[i] != be_ref[jnp.maximum(i - 1, 0)]))
        def _():
            wgu_b[...] = wgu_ref[...].astype(BF16)
            wdn_b[...] = wdn_ref[...].astype(BF16)

        _wait_row_gather(h2_hbm, MOE_ROWS, xbuf.at[slot], sem.at[slot])
        xb[...] = xbuf[slot].astype(BF16)
        cw = D_FF // FF_CHUNKS
        for c in range(FF_CHUNKS):
            gsl = slice(c * cw, (c + 1) * cw)
            usl = slice(D_FF + c * cw, D_FF + (c + 1) * cw)
            gate = jnp.dot(xb[...], wgu_b[:, gsl], preferred_element_type=F32) + bgu_ref[:, gsl]
            up = jnp.dot(xb[...], wgu_b[:, usl], preferred_element_type=F32) + bgu_ref[:, usl]
            gate = jnp.minimum(gate, SWIGLU_LIMIT)
            up = jnp.clip(up, -SWIGLU_LIMIT, SWIGLU_LIMIT)
            act[:, gsl] = ((up + 1.0) * (gate * _sigmoid(SWIGLU_ALPHA * gate))).astype(BF16)
        ys_ref[...] = jnp.dot(act[...], wdn_b[...], preferred_element_type=F32) + bdn_ref[...]

    @pl.when(i >= n_used)
    def _():
        ys_ref[...] = jnp.zeros_like(ys_ref)


def _moe(h2, block_expert, block_cstart, n_used, toks, w_gu, b_gu, w_dn, b_dn, layer):
    n_blocks = block_expert.shape[0]
    idx = lambda i, be, cs, nu, tk: (layer, be[i], 0, 0)
    grid_spec = pltpu.PrefetchScalarGridSpec(
        num_scalar_prefetch=4,
        grid=(n_blocks,),
        in_specs=[
            pl.BlockSpec(memory_space=pl.ANY),
            pl.BlockSpec((None, None, D_MODEL, 2 * D_FF), idx),
            pl.BlockSpec((None, None, 1, 2 * D_FF), idx),
            pl.BlockSpec((None, None, D_FF, D_MODEL), idx),
            pl.BlockSpec((None, None, 1, D_MODEL), idx),
        ],
        out_specs=pl.BlockSpec((MOE_ROWS, D_MODEL), lambda i, be, cs, nu, tk: (i, 0)),
        scratch_shapes=[
            pltpu.VMEM((2, MOE_ROWS, D_MODEL), F32),
            pltpu.SemaphoreType.DMA((2,)),
            pltpu.VMEM((MOE_ROWS, D_MODEL), BF16),
            pltpu.VMEM((MOE_ROWS, D_FF), BF16),
            pltpu.VMEM((D_MODEL, 2 * D_FF), BF16),
            pltpu.VMEM((D_FF, D_MODEL), BF16),
        ],
    )
    depth = w_gu.shape[0]
    return pl.pallas_call(
        _moe_kernel,
        grid_spec=grid_spec,
        out_shape=jax.ShapeDtypeStruct((n_blocks * MOE_ROWS, D_MODEL), F32),
        compiler_params=_params(("arbitrary",)),
        name="moe",
    )(block_expert, block_cstart, n_used, toks, h2, w_gu,
      b_gu.reshape(depth, N_EXPERTS, 1, 2 * D_FF), w_dn, b_dn.reshape(depth, N_EXPERTS, 1, D_MODEL))


def _route(meta, counts_f):
    m = meta.shape[0]
    a = m * TOP_K
    counts = counts_f[0, :N_EXPERTS].astype(jnp.int32)
    expert = meta[:, 0:TOP_K].astype(jnp.int32)
    rank = meta[:, 2 * TOP_K:3 * TOP_K].astype(jnp.int32)
    pad_counts = (counts + MOE_ROWS - 1) // MOE_ROWS * MOE_ROWS
    start = jnp.cumsum(counts) - counts
    pad_end = jnp.cumsum(pad_counts)
    pad_start = pad_end - pad_counts
    slot_of = jnp.take(pad_start, expert) + rank
    _, toks = lax.sort((slot_of.reshape(-1), jnp.arange(a, dtype=jnp.int32) // TOP_K), num_keys=1)
    toks = jnp.concatenate([toks, jnp.zeros((MOE_ROWS,), jnp.int32)])
    n_blocks = (a + N_EXPERTS * (MOE_ROWS - 1) + MOE_ROWS - 1) // MOE_ROWS
    n_used = (pad_end[-1] // MOE_ROWS).astype(jnp.int32)
    block_start = jnp.arange(n_blocks, dtype=jnp.int32) * MOE_ROWS
    used = jnp.arange(n_blocks) < n_used
    block_expert = jnp.minimum(jnp.sum(block_start[:, None] >= pad_end[None, :], axis=1),
                               N_EXPERTS - 1).astype(jnp.int32)
    block_cstart = jnp.where(used, start[block_expert] + block_start - pad_start[block_expert], 0)
    last_expert = block_expert[jnp.maximum(n_used - 1, 0)]
    block_expert = jnp.where(used, block_expert, last_expert)
    return block_expert, block_cstart.astype(jnp.int32), n_used.reshape(1), toks, slot_of


def _combine_kernel(sl_cur, sl_nxt, ys_hbm, x1_ref, meta_ref, pg2, sg2, *rest, tiles_p, tm, final):
    if final:
        fpsh, fpsc, fssh, fssc, fg_ref, yp_ref, ysm_ref, buf, sem = rest
    else:
        x2_ref, buf, sem = rest
    i = pl.program_id(0)
    n = pl.num_programs(0)
    slot = i % 2
    rows = TOP_K * tm

    @pl.when(i == 0)
    def _():
        _start_row_gather(ys_hbm, lambda r: sl_cur[0, r], rows, buf.at[0], sem.at[0], GATHER_UNROLL)

    @pl.when(i + 1 < n)
    def _():
        _start_row_gather(ys_hbm, lambda r: sl_nxt[0, r], rows, buf.at[1 - slot], sem.at[1 - slot],
                          GATHER_UNROLL)

    _wait_row_gather(ys_hbm, rows, buf.at[slot], sem.at[slot])
    meta = meta_ref[...]
    ff = meta[:, TOP_K:TOP_K + 1] * buf[slot, pl.ds(0, tm), :]
    for kk in range(1, TOP_K):
        ff = ff + meta[:, TOP_K + kk:TOP_K + kk + 1] * buf[slot, pl.ds(kk * tm, tm), :]
    is_s = i >= tiles_p
    x2 = x1_ref[...] + _pick(is_s, sg2, pg2) * ff
    if not final:
        x2_ref[...] = x2
        return
    y = _rms(x2) * fg_ref[...] * (1.0 + _pick(is_s, fssc, fpsc)) + _pick(is_s, fssh, fpsh)

    @pl.when(i < tiles_p)
    def _():
        yp_ref[...] = y

    @pl.when(is_s)
    def _():
        ysm_ref[...] = y


def _combine(tok, ys, slot_of, x1, meta, modp, mods, layer, final=None):
    tm = tok.tm
    sl3 = slot_of.reshape(tok.tiles, tm, TOP_K).transpose(0, 2, 1).reshape(tok.tiles, 1, TOP_K * tm)
    smem = lambda f: pl.BlockSpec((None, 1, TOP_K * tm), f, memory_space=pltpu.SMEM)
    last = tok.tiles - 1
    in_specs = [smem(lambda i: (i, 0, 0)), smem(lambda i: (jnp.minimum(i + 1, last), 0, 0)),
                pl.BlockSpec(memory_space=pl.ANY),
                tok.rows_spec(D_MODEL), tok.rows_spec(LANES),
                tok.modp_spec(layer, 5), tok.mods_spec(layer, 5)]
    args = [sl3, sl3, ys, x1, meta, modp, mods]
    if final is None:
        out_specs = tok.rows_spec(D_MODEL)
        out_shape = jax.ShapeDtypeStruct((tok.m, D_MODEL), F32)
    else:
        fmodp, fmods, final_g = final
        in_specs += [tok.modp_spec(0, 0), tok.modp_spec(0, 1), tok.mods_spec(0, 0), tok.mods_spec(0, 1),
                     pl.BlockSpec((1, D_MODEL), lambda i: (0, 0))]
        args += [fmodp, fmodp, fmods, fmods, final_g.reshape(1, D_MODEL)]
        out_specs = list(tok.split_specs(D_MODEL))
        out_shape = [jax.ShapeDtypeStruct((tok.mp, D_MODEL), F32),
                     jax.ShapeDtypeStruct((tok.ms, D_MODEL), F32)]
    return pl.pallas_call(
        functools.partial(_combine_kernel, tiles_p=tok.tiles_p, tm=tm, final=final is not None),
        grid=(tok.tiles,),
        in_specs=in_specs,
        out_specs=out_specs,
        out_shape=out_shape,
        scratch_shapes=[pltpu.VMEM((2, TOP_K * tm, D_MODEL), F32),
                        pltpu.SemaphoreType.DMA((2,))],
        compiler_params=_params(("arbitrary",)),
        name="combine",
    )(*args)


def kernel(x_prompt, x_sample, c_prompt, c_sample, state_hgrn, state_ret, ada_w, ada_b, norm1_g, norm2_g, w_in, lb_logits, gn_hgrn, gn_ret, w_out, router_w, router_b, w_gate_up, b_gate_up, w_down, b_down, final_g, final_ada_w, final_ada_b):
    bp, seq, _ = x_prompt.shape
    bs = x_sample.shape[0]
    depth = ada_w.shape[0]
    assert x_sample.shape[1] == DEC_SEQ and seq % CHUNK == 0 and (bs * DEC_SEQ) % CHUNK == 0
    ms = bs * DEC_SEQ
    tm = 256 if (ms % 256 == 0 and seq % 256 == 0) else CHUNK
    tok = _Tok(bp, seq, bs, tm)

    c_all = jnp.concatenate([jnp.repeat(c_sample, DEC_SEQ, axis=0), c_prompt], axis=0)
    mod = _ada(c_all, ada_w, ada_b)
    fmod = _ada(c_all, final_ada_w[None], final_ada_b[None])
    modp = mod[:, ms:].reshape(depth, bp, 1, 6 * D_MODEL)
    fmodp = fmod[:, ms:].reshape(1, bp, 1, 2 * D_MODEL)

    lb_sm = jax.nn.softmax(lb_logits.astype(F32), axis=0)
    lb_all = jnp.cumsum(lb_sm, axis=0) - lb_sm[0]
    w_in_b = w_in.astype(BF16)
    w_out_b = w_out.astype(BF16)
    rw_pad = jnp.pad(router_w, ((0, 0), (0, 0), (0, LANES - N_EXPERTS)))
    rb_pad = jnp.pad(router_b, ((0, 0), (0, LANES - N_EXPERTS)))

    xp = x_prompt.reshape(tok.mp, D_MODEL)
    xs = x_sample.reshape(tok.ms, D_MODEL)
    x_specs = tok.split_specs(D_MODEL)
    new_a_p, new_a_s, new_r_p, new_r_s = [], [], [], []
    for l in range(depth):
        proj = _proj(tok, xp, xs, x_specs, modp, mod, l, norm1_g[l], w_in_b[l])
        oa_p, oa_s, sa_p, sa_s = _hgrn(tok, proj, lb_all[l], gn_hgrn[l], state_hgrn[l])
        or_p, or_s, sr_p, sr_s = _ret(tok, proj, gn_ret[l], state_ret[l])
        x1, h2, meta, counts = _post(tok, xp, xs, x_specs, oa_p, oa_s, or_p, or_s, w_out_b[l],
                                     modp, mod, l, norm2_g[l], rw_pad[l], rb_pad[l].reshape(1, LANES))
        block_expert, block_cstart, n_used, toks, slot_of = _route(meta, counts)
        ys = _moe(h2, block_expert, block_cstart, n_used, toks, w_gate_up, b_gate_up, w_down, b_down, l)
        new_a_p.append(sa_p)
        new_a_s.append(sa_s)
        new_r_p.append(sr_p)
        new_r_s.append(sr_s)
        if l == depth - 1:
            y_p, y_s = _combine(tok, ys, slot_of, x1, meta, modp, mod, l, final=(fmodp, fmod, final_g))
            break
        xp = xs = _combine(tok, ys, slot_of, x1, meta, modp, mod, l)
        tp = tok.tiles_p
        x_specs = (pl.BlockSpec((tm, D_MODEL), lambda i: (jnp.minimum(i, tp - 1), 0)),
                   pl.BlockSpec((tm, D_MODEL), lambda i: (jnp.maximum(i, tp), 0)))

    return (y_p.reshape(bp, seq, D_MODEL), y_s.reshape(bs, DEC_SEQ, D_MODEL),
            jnp.stack(new_a_p), jnp.stack(new_r_p), jnp.stack(new_a_s), jnp.stack(new_r_s))
```
